```python
import math
import jax, jax.numpy as jnp
from jax import lax
import numpy as np

D_MODEL = 4096
BATCH = 8
SEQ = 2048
DEPTH = 2

CHUNK = 64
Q_BLOCK = 128
HEAD_DIM = 128
N_HEAD_SLOTS = D_MODEL // HEAD_DIM
A_HEADS = N_HEAD_SLOTS // 4
A_DK = HEAD_DIM
A_DV = HEAD_DIM
B_HEADS = 3 * N_HEAD_SLOTS // 8
B_DK = HEAD_DIM
B_DV = HEAD_DIM
C_HEADS = 3 * N_HEAD_SLOTS // 8
C_DQK = HEAD_DIM // 2
C_DV = HEAD_DIM
A_W = A_HEADS * A_DV
B_W = B_HEADS * B_DV
C_W = C_HEADS * C_DV
CONV_K = 4
ROPE_THETA = 500000.0
ROPE_DIM = C_DQK // 4
N_GROUPS = 4
EXPERTS_PER_GROUP = 8
EXPERT_FF = D_MODEL // 8
TOP_K = 2
EPS = 1e-6

IN_SPLITS = (A_HEADS * A_DK, A_HEADS * A_DK, A_W, A_W,
             B_HEADS * B_DK, B_HEADS * B_DK, B_W, B_W, B_HEADS, B_HEADS,
             C_HEADS * 2 * C_DQK, C_HEADS * 2 * C_DQK, C_W,
             3 * D_MODEL)
IN_WIDTH = sum(IN_SPLITS)

kernel_name = 'hybrid_hgrn2_mlstm_diffattn_hiermoe'


def rms_norm(x, g):
    xf = x.astype(jnp.float32)
    y = xf * lax.rsqrt(jnp.mean(xf * xf, axis=-1, keepdims=True) + EPS)
    return (y * g.astype(jnp.float32)).astype(x.dtype)


def causal_conv(x, w):
    c = x.shape[-1]
    return lax.conv_general_dilated(x, w[:, None, :].astype(x.dtype), window_strides=(1,),
                                    padding=[(CONV_K - 1, 0)],
                                    dimension_numbers=('NWC', 'WIO', 'NWC'),
                                    feature_group_count=c)


def partial_rope(t, cos, sin):
    half = ROPE_DIM // 2
    c = cos[:, None, None, :].astype(t.dtype)
    s = sin[:, None, None, :].astype(t.dtype)
    t1 = t[..., :half]
    t2 = t[..., half:ROPE_DIM]
    return jnp.concatenate([t1 * c - t2 * s, t2 * c + t1 * s, t[..., ROPE_DIM:]], axis=-1)


def hgrn2_mixer(q, f_pre, i, g, lb, norm_g):
    bsz, s_len, _ = q.shape
    n = s_len // CHUNK
    f32 = jnp.float32

    def heads(t, d):
        return t.astype(f32).reshape(bsz, n, CHUNK, A_HEADS, d).transpose(1, 0, 3, 2, 4)

    z = f_pre.astype(f32)
    lbf = lb.astype(f32)
    log_f = jnp.log(lbf + (1.0 - lbf) * jax.nn.sigmoid(z))
    k = (1.0 - lbf) * jax.nn.sigmoid(-z)
    qh = heads(jax.nn.silu(q.astype(f32)), A_DK)
    kh = heads(k, A_DK)
    vh = heads(i, A_DV)
    bcum = jnp.cumsum(heads(log_f, A_DK), axis=3)
    tri = jnp.tril(jnp.ones((CHUNK, CHUNK), bool))

    def step(state, xs):
        qc, kc, vc, bc = xs
        diff = bc[:, :, :, None, :] - bc[:, :, None, :, :]
        decay = jnp.exp(jnp.where(tri[:, :, None], diff, -jnp.inf))
        attn = jnp.einsum('bhtd,bhsd,bhtsd->bhts', qc, kc, decay)
        o = (jnp.einsum('bhts,bhsv->bhtv', attn, vc)
             + jnp.einsum('bhtd,bhdv->bhtv', qc * jnp.exp(bc), state))
        b_last = bc[:, :, -1:, :]
        new_state = (jnp.exp(b_last[:, :, 0, :])[..., None] * state
                     + jnp.einsum('bhsd,bhsv->bhdv', kc * jnp.exp(b_last - bc), vc))
        return new_state, o

    s0 = jnp.zeros((bsz, A_HEADS, A_DK, A_DV), f32)
    _, o = lax.scan(step, s0, (qh, kh, vh, bcum))
    o = o.transpose(1, 0, 3, 2, 4).reshape(bsz, s_len, A_HEADS, A_DV)
    o = rms_norm(o, norm_g).reshape(bsz, s_len, A_W) * jax.nn.silu(g.astype(f32))
    return o.astype(q.dtype)


def mlstm_mixer(q, k, v, o_pre, i_pre, f_pre, norm_g):
    bsz, s_len, _ = q.shape
    n = s_len // CHUNK
    f32 = jnp.float32

    def heads(t, d):
        return t.astype(f32).reshape(bsz, n, CHUNK, B_HEADS, d).transpose(1, 0, 3, 2, 4)

    def gate(t):
        return t.astype(f32).reshape(bsz, n, CHUNK, B_HEADS).transpose(1, 0, 3, 2)

    qh = heads(q, B_DK) * (B_DK ** -0.5)
    kh = heads(k, B_DK)
    vh = heads(v, B_DV)
    li = gate(i_pre)
    fcum = jnp.cumsum(jax.nn.log_sigmoid(gate(f_pre)), axis=-1)
    tri = jnp.tril(jnp.ones((CHUNK, CHUNK), bool))

    def step(carry, xs):
        c_prev, n_prev, m_prev = carry
        qc, kc, vc, lic, fc = xs
        log_d = jnp.where(tri, fc[..., :, None] - fc[..., None, :] + lic[..., None, :], -jnp.inf)
        inter = fc + m_prev[..., None]
        m = jnp.maximum(jnp.max(log_d, axis=-1), inter)
        dmat = jnp.exp(log_d - m[..., None])
        a = jnp.exp(inter - m)
        sc = jnp.einsum('bhtd,bhsd->bhts', qc, kc) * dmat
        num = (jnp.einsum('bhts,bhsv->bhtv', sc, vc)
               + a[..., None] * jnp.einsum('bhtd,bhdv->bhtv', qc, c_prev))
        den = jnp.sum(sc, axis=-1) + a * jnp.einsum('bhtd,bhd->bht', qc, n_prev)
        h = num / jnp.maximum(jnp.abs(den), jnp.exp(-m))[..., None]
        f_last = fc[..., -1]
        log_w = f_last[..., None] - fc + lic
        m_new = jnp.maximum(f_last + m_prev, jnp.max(log_w, axis=-1))
        dec = jnp.exp(f_last + m_prev - m_new)
        w = jnp.exp(log_w - m_new[..., None])
        c_new = dec[..., None, None] * c_prev + jnp.einsum('bhs,bhsd,bhsv->bhdv', w, kc, vc)
        n_new = dec[..., None] * n_prev + jnp.einsum('bhs,bhsd->bhd', w, kc)
        return (c_new, n_new, m_new), h

    carry0 = (jnp.zeros((bsz, B_HEADS, B_DK, B_DV), f32),
              jnp.zeros((bsz, B_HEADS, B_DK), f32),
              jnp.zeros((bsz, B_HEADS), f32))
    _, h = lax.scan(step, carry0, (qh, kh, vh, li, fcum))
    h = h.transpose(1, 0, 3, 2, 4).reshape(bsz, s_len, B_HEADS, B_DV)
    h = rms_norm(h, norm_g).reshape(bsz, s_len, B_W) * jax.nn.sigmoid(o_pre.astype(f32))
    return h.astype(q.dtype)


def diff_attention(q, k, v, qn_g, kn_g, lam_q1, lam_k1, lam_q2, lam_k2, subln_g, lambda_init, cos, sin):
    bsz, s_len, _ = q.shape
    f32 = jnp.float32
    qh = partial_rope(rms_norm(q.reshape(bsz, s_len, C_HEADS, 2, C_DQK), qn_g), cos, sin)
    kh = partial_rope(rms_norm(k.reshape(bsz, s_len, C_HEADS, 2, C_DQK), kn_g), cos, sin)
    qh = qh.transpose(0, 2, 3, 1, 4)
    kh = kh.transpose(0, 2, 3, 1, 4)
    vh = v.reshape(bsz, s_len, C_HEADS, C_DV).transpose(0, 2, 1, 3).astype(f32)
    lam = (jnp.exp(jnp.sum(lam_q1.astype(f32) * lam_k1.astype(f32)))
           - jnp.exp(jnp.sum(lam_q2.astype(f32) * lam_k2.astype(f32))) + lambda_init)
    scale = C_DQK ** -0.5
    chunk_id = jnp.arange(s_len) // CHUNK
    outs = []
    for blk in range(s_len // Q_BLOCK):
        q0 = blk * Q_BLOCK
        kend = q0 + Q_BLOCK
        scores = jnp.einsum('bhmqd,bhmkd->bhmqk', qh[:, :, :, q0:kend], kh[:, :, :, :kend]).astype(f32) * scale
        allowed = chunk_id[q0:kend, None] >= chunk_id[None, :kend]
        p = jax.nn.softmax(jnp.where(allowed, scores, -jnp.inf), axis=-1)
        pd = p[:, :, 0] - lam * p[:, :, 1]
        outs.append(jnp.einsum('bhqk,bhkv->bhqv', pd, vh[:, :, :kend]))
    o = jnp.concatenate(outs, axis=2)
    o = rms_norm(o, subln_g) * (1.0 - lambda_init)
    return o.transpose(0, 2, 1, 3).reshape(bsz, s_len, C_W).astype(q.dtype)


def hier_moe(xn, wg, bg, we, be, w1, w3, w2):
    bsz, s_len, d = xn.shape
    f32 = jnp.float32
    xt = xn.reshape(bsz * s_len, d)
    g_logits = (xt @ wg + bg).astype(f32)
    g_prob = jax.nn.softmax(g_logits, axis=-1)
    g_idx = jnp.argmax(g_logits, axis=-1)
    g_w = jnp.take_along_axis(g_prob, g_idx[:, None], axis=-1)
    e_logits = (xt @ we + be).astype(f32).reshape(-1, N_GROUPS, EXPERTS_PER_GROUP)
    e_sel = jnp.take_along_axis(e_logits, g_idx[:, None, None], axis=1)[:, 0]
    top_v, top_i = lax.top_k(e_sel, TOP_K)
    top_w = jax.nn.softmax(top_v, axis=-1) * g_w
    within = jnp.sum(jax.nn.one_hot(top_i, EXPERTS_PER_GROUP, dtype=f32) * top_w[..., None], axis=1)
    comb = jax.nn.one_hot(g_idx, N_GROUPS, dtype=f32)[:, :, None] * within[:, None, :]
    out = jnp.zeros_like(xt)
    for grp in range(N_GROUPS):
        hg = (jax.nn.silu(jnp.einsum('td,edf->tef', xt, w1[grp]))
              * jnp.einsum('td,edf->tef', xt, w3[grp]))
        hg = hg * comb[:, grp, :, None].astype(hg.dtype)
        out = out + jnp.einsum('tef,efd->td', hg, w2[grp])
    return out.reshape(bsz, s_len, d)


def setup_inputs(seed: int = 0) -> dict:
    key = jax.random.key(seed)
    ks = jax.random.split(key, 32)

    def nrm(k, shape, scale):
        return jax.random.normal(k, shape, jnp.float32) * scale

    ne = N_GROUPS * EXPERTS_PER_GROUP
    return {
        'x': nrm(ks[0], (BATCH, SEQ, D_MODEL), 1.0),
        'norm1_g': 1.0 + nrm(ks[1], (DEPTH, D_MODEL), 0.02),
        'w_in': nrm(ks[2], (DEPTH, D_MODEL, IN_WIDTH), D_MODEL ** -0.5),
        'hgrn_lb': nrm(ks[3], (DEPTH, A_HEADS * A_DK), 0.5),
        'hgrn_norm_g': 1.0 + nrm(ks[4], (DEPTH, A_DV), 0.02),
        'mlstm_conv_w': nrm(ks[5], (DEPTH, CONV_K, 2 * B_HEADS * B_DK), CONV_K ** -0.5),
        'mlstm_i_bias': nrm(ks[6], (DEPTH, B_HEADS), 0.1),
        'mlstm_f_bias': jnp.linspace(3.0, 6.0, B_HEADS, dtype=jnp.float32)[None, :] + nrm(ks[7], (DEPTH, B_HEADS), 0.1),
        'mlstm_norm_g': 1.0 + nrm(ks[8], (DEPTH, B_DV), 0.02),
        'diff_qn_g': 1.0 + nrm(ks[9], (DEPTH, C_DQK), 0.02),
        'diff_kn_g': 1.0 + nrm(ks[10], (DEPTH, C_DQK), 0.02),
        'diff_lam_q1': nrm(ks[11], (DEPTH, C_DQK), 0.1),
        'diff_lam_k1': nrm(ks[12], (DEPTH, C_DQK), 0.1),
        'diff_lam_q2': nrm(ks[13], (DEPTH, C_DQK), 0.1),
        'diff_lam_k2': nrm(ks[14], (DEPTH, C_DQK), 0.1),
        'diff_subln_g': 1.0 + nrm(ks[15], (DEPTH, C_DV), 0.02),
        'w_branch_a': nrm(ks[16], (DEPTH, A_W, D_MODEL), A_W ** -0.5),
        'w_branch_b': nrm(ks[17], (DEPTH, B_W, D_MODEL), B_W ** -0.5),
        'w_branch_c': nrm(ks[18], (DEPTH, C_W, D_MODEL), C_W ** -0.5),
        'w_out': nrm(ks[19], (DEPTH, D_MODEL, D_MODEL), D_MODEL ** -0.5),
        'norm2_g': 1.0 + nrm(ks[20], (DEPTH, D_MODEL), 0.02),
        'router_group_w': nrm(ks[21], (DEPTH, D_MODEL, N_GROUPS), D_MODEL ** -0.5),
        'router_group_b': nrm(ks[22], (DEPTH, N_GROUPS), 0.01),
        'router_expert_w': nrm(ks[23], (DEPTH, D_MODEL, ne), D_MODEL ** -0.5),
        'router_expert_b': nrm(ks[24], (DEPTH, ne), 0.01),
        'expert_w1': nrm(ks[25], (DEPTH, N_GROUPS, EXPERTS_PER_GROUP, D_MODEL, EXPERT_FF), D_MODEL ** -0.5),
        'expert_w3': nrm(ks[26], (DEPTH, N_GROUPS, EXPERTS_PER_GROUP, D_MODEL, EXPERT_FF), D_MODEL ** -0.5),
        'expert_w2': nrm(ks[27], (DEPTH, N_GROUPS, EXPERTS_PER_GROUP, EXPERT_FF, D_MODEL), EXPERT_FF ** -0.5),
    }


def reference(x, norm1_g, w_in, hgrn_lb, hgrn_norm_g, mlstm_conv_w, mlstm_i_bias, mlstm_f_bias,
              mlstm_norm_g, diff_qn_g, diff_kn_g, diff_lam_q1, diff_lam_k1, diff_lam_q2, diff_lam_k2,
              diff_subln_g, w_branch_a, w_branch_b, w_branch_c, w_out, norm2_g, router_group_w,
              router_group_b, router_expert_w, router_expert_b, expert_w1, expert_w3, expert_w2):
    f32 = jnp.float32
    s_len = x.shape[1]
    pos = jnp.arange(s_len, dtype=f32)
    inv_freq = ROPE_THETA ** (-jnp.arange(0, ROPE_DIM, 2, dtype=f32) / ROPE_DIM)
    ang = pos[:, None] * inv_freq[None, :]
    cos, sin = jnp.cos(ang), jnp.sin(ang)
    p_lb = jax.nn.softmax(hgrn_lb.astype(f32), axis=0)
    lb_all = jnp.cumsum(p_lb, axis=0) - p_lb[0:1]
    split_idx = [int(v) for v in np.cumsum(IN_SPLITS)[:-1]]

    for l in range(DEPTH):
        xn = rms_norm(x, norm1_g[l])
        h = xn @ w_in[l]
        (aq, af, ai, ag, bq, bk, bv, bo, bi, bf, cq, ck, cv, gates) = jnp.split(h, split_idx, axis=-1)
        out_a = hgrn2_mixer(aq, af, ai, ag, lb_all[l], hgrn_norm_g[l])
        bqk = jax.nn.silu(causal_conv(jnp.concatenate([bq, bk], axis=-1), mlstm_conv_w[l]))
        bq_c, bk_c = jnp.split(bqk, 2, axis=-1)
        out_b = mlstm_mixer(bq_c, bk_c, bv, bo, bi + mlstm_i_bias[l], bf + mlstm_f_bias[l], mlstm_norm_g[l])
        lambda_init = 0.8 - 0.6 * math.exp(-0.3 * l)
        out_c = diff_attention(cq, ck, cv, diff_qn_g[l], diff_kn_g[l], diff_lam_q1[l], diff_lam_k1[l],
                               diff_lam_q2[l], diff_lam_k2[l], diff_subln_g[l], lambda_init, cos, sin)
        ga, gb, gc = jnp.split(jax.nn.sigmoid(gates), 3, axis=-1)
        y = ga * (out_a @ w_branch_a[l]) + gb * (out_b @ w_branch_b[l]) + gc * (out_c @ w_branch_c[l])
        x = x + y @ w_out[l]
        xn2 = rms_norm(x, norm2_g[l])
        x = x + hier_moe(xn2, router_group_w[l], router_group_b[l], router_expert_w[l], router_expert_b[l],
                         expert_w1[l], expert_w3[l], expert_w2[l])
    return x
```

```python
import functools
import math

import jax
import jax.numpy as jnp
from jax import lax
from jax.experimental import pallas as pl
from jax.experimental.pallas import tpu as pltpu

F32 = jnp.float32
BF16 = jnp.bfloat16

HEAD_DIM = 128
A_HEADS = 8
B_HEADS = 12
C_HEADS = 12
C_DQK = 64
CONV_K = 4
ROPE_THETA = 500000.0
ROPE_DIM = C_DQK // 4
ROPE_HALF = ROPE_DIM // 2
N_GROUPS = 4
EXPERTS_PER_GROUP = 8
N_EXPERTS = N_GROUPS * EXPERTS_PER_GROUP
EPS = 1e-6
LANES = 128
NEG_BIG = -1e30

VMEM_LIMIT = 56 * 1024 * 1024

HGRN_CHUNK = 128
HGRN_SUB = 32
HGRN_EXP_CLAMP = 60.0
MLSTM_CHUNK = 128
ATT_BLOCK = 128
MASK_CHUNK = 64
MOE_TILE = 256
COMBINE_TILE = 128


def _cparams(*sem):
    return pltpu.CompilerParams(dimension_semantics=sem, vmem_limit_bytes=VMEM_LIMIT)


def _sigmoid(x):
    return 1.0 / (1.0 + jnp.exp(-x))


def _dot(a, b):
    return jnp.dot(a, b, preferred_element_type=F32)


def _dot_nt(a, b):
    return lax.dot_general(a, b, (((1,), (1,)), ((), ())), preferred_element_type=F32)


def _dot_tn(a, b):
    return lax.dot_general(a, b, (((0,), (0,)), ((), ())), preferred_element_type=F32)


def _split_dot(exact_lhs, x):
    hi = x.astype(BF16)
    lo = (x - hi.astype(F32)).astype(BF16)
    return _dot(exact_lhs, hi) + _dot(exact_lhs, lo)


def _split_dot_rhs(x, exact_rhs):
    hi = x.astype(BF16)
    lo = (x - hi.astype(F32)).astype(BF16)
    return _dot(hi, exact_rhs) + _dot(lo, exact_rhs)


def _rmsnorm_kernel(x_ref, g_ref, o_ref):
    x = x_ref[...]
    y = x * lax.rsqrt(jnp.mean(x * x, axis=-1, keepdims=True) + EPS) * g_ref[...]
    o_ref[...] = y.astype(o_ref.dtype)


def _rmsnorm(x, g, out_dtype, tm=256):
    t, d = x.shape
    return pl.pallas_call(
        _rmsnorm_kernel,
        grid=(t // tm,),
        in_specs=[pl.BlockSpec((tm, d), lambda i: (i, 0)), pl.BlockSpec((1, d), lambda i: (0, 0))],
        out_specs=pl.BlockSpec((tm, d), lambda i: (i, 0)),
        out_shape=jax.ShapeDtypeStruct((t, d), out_dtype),
        compiler_params=_cparams("parallel"),
        name="rmsnorm",
    )(x, g.reshape(1, d))


def _mm_kernel(x_ref, w_ref, o_ref):
    o_ref[...] = _dot(x_ref[...], w_ref[...]).astype(o_ref.dtype)


def _mm_res_kernel(x_ref, w_ref, r_ref, o_ref):
    o_ref[...] = (r_ref[...] + _dot(x_ref[...], w_ref[...])).astype(o_ref.dtype)


def _pick_tile(n, candidates):
    for c in candidates:
        if n % c == 0:
            return c
    return n


def _matmul(x, w, out_dtype, residual=None, name="matmul"):
    m, k = x.shape
    n = w.shape[1]
    tm = _pick_tile(m, (512, 256, 128))
    tn = _pick_tile(n, (1024, 768, 512, 256, 128))
    in_specs = [pl.BlockSpec((tm, k), lambda j, i: (i, 0)), pl.BlockSpec((k, tn), lambda j, i: (0, j))]
    args = [x, w]
    kern = _mm_kernel
    if residual is not None:
        in_specs.append(pl.BlockSpec((tm, tn), lambda j, i: (i, j)))
        args.append(residual)
        kern = _mm_res_kernel
    return pl.pallas_call(
        kern,
        grid=(n // tn, m // tm),
        in_specs=in_specs,
        out_specs=pl.BlockSpec((tm, tn), lambda j, i: (i, j)),
        out_shape=jax.ShapeDtypeStruct((m, n), out_dtype),
        compiler_params=_cparams("parallel", "parallel"),
        name=name,
    )(*args)


def _hgrn2_kernel(q_ref, f_ref, i_ref, g_ref, lb_ref, ng_ref, o_ref, st_ref):
    c = pl.program_id(2)
    ch, sb = HGRN_CHUNK, HGRN_SUB
    nsb = ch // sb

    @pl.when(c == 0)
    def _():
        st_ref[...] = jnp.zeros_like(st_ref)

    z = f_ref[0].astype(F32)
    lb = lb_ref[...]
    sig = _sigmoid(z)
    logf = jnp.log(lb + (1.0 - lb) * sig)
    k = (1.0 - lb) * (1.0 - sig)
    qp = q_ref[0].astype(F32)
    q = qp * _sigmoid(qp)
    v = i_ref[0]

    row = lax.broadcasted_iota(jnp.int32, (ch, ch), 0)
    col = lax.broadcasted_iota(jnp.int32, (ch, ch), 1)
    causal = row >= col
    tri = jnp.where(causal, 1.0, 0.0).astype(BF16)
    b = _split_dot(tri, logf)
    b_last = b[ch - 1:ch, :]

    st = st_ref[...]
    o = _dot_nt((q * jnp.exp(b)).astype(BF16), st.astype(BF16))

    refs = [b[i * sb:i * sb + 1, :] for i in range(nsb)]
    rblk = jnp.concatenate([jnp.broadcast_to(r, (sb, HEAD_DIM)) for r in refs], axis=0)
    qt = q * jnp.exp(b - rblk)
    rowblk = lax.broadcasted_iota(jnp.int32, (ch, HEAD_DIM), 0) // sb
    qcat = jnp.concatenate([jnp.where(rowblk == i, qt, 0.0).astype(BF16) for i in range(nsb)], axis=1)
    kcat = jnp.concatenate(
        [(k * jnp.exp(jnp.minimum(r - b, HGRN_EXP_CLAMP))).astype(BF16) for r in refs], axis=1)
    s = jnp.where(causal, _dot_nt(qcat, kcat), 0.0)
    o = o + _dot(s.astype(BF16), v)

    kd = (k * jnp.exp(b_last - b)).astype(BF16)
    st_ref[...] = st * jnp.exp(b_last) + _dot_tn(v, kd)

    y = o * lax.rsqrt(jnp.mean(o * o, axis=-1, keepdims=True) + EPS) * ng_ref[...]
    gp = g_ref[0].astype(F32)
    o_ref[0] = (y * (gp * _sigmoid(gp))).astype(o_ref.dtype)


def _hgrn2(h_a, lb, norm_g):
    bsz, s_len, _ = h_a.shape
    ch = HGRN_CHUNK

    def col(j):
        return pl.BlockSpec((1, ch, HEAD_DIM), lambda b, h, c, j=j: (b, c, j * A_HEADS + h))

    return pl.pallas_call(
        _hgrn2_kernel,
        grid=(bsz, A_HEADS, s_len // ch),
        in_specs=[col(0), col(1), col(2), col(3),
                  pl.BlockSpec((1, HEAD_DIM), lambda b, h, c: (0, h)),
                  pl.BlockSpec((1, HEAD_DIM), lambda b, h, c: (0, 0))],
        out_specs=pl.BlockSpec((1, ch, HEAD_DIM), lambda b, h, c: (b, c, h)),
        out_shape=jax.ShapeDtypeStruct((bsz, s_len, A_HEADS * HEAD_DIM), BF16),
        scratch_shapes=[pltpu.VMEM((HEAD_DIM, HEAD_DIM), F32)],
        compiler_params=_cparams("parallel", "parallel", "arbitrary"),
        name="hgrn2",
    )(h_a, h_a, h_a, h_a, lb.reshape(1, -1), norm_g.reshape(1, HEAD_DIM))


def _log_sigmoid(x):
    return jnp.minimum(x, 0.0) - jnp.log(1.0 + jnp.exp(-jnp.abs(x)))


def _causal_conv_silu(x, prev, w):
    ch = x.shape[0]
    rowi = lax.broadcasted_iota(jnp.int32, x.shape, 0)
    acc = x * w[CONV_K - 1:CONV_K, :]
    for sft in range(1, CONV_K):
        shifted = jnp.where(rowi >= sft, pltpu.roll(x, sft, 0), pltpu.roll(prev, sft, 0))
        acc = acc + shifted * w[CONV_K - 1 - sft:CONV_K - sft, :]
    return acc * _sigmoid(acc)


def _mlstm_kernel(q_ref, k_ref, v_ref, og_ref, wq_ref, wk_ref, gir_ref, gfr_ref, gic_ref, gfc_ref, ng_ref,
                  o_ref, c_ref, n_ref, m_ref, qprev_ref, kprev_ref):
    c = pl.program_id(2)
    ch = MLSTM_CHUNK

    @pl.when(c == 0)
    def _():
        c_ref[...] = jnp.zeros_like(c_ref)
        n_ref[...] = jnp.zeros_like(n_ref)
        m_ref[...] = jnp.zeros_like(m_ref)
        qprev_ref[...] = jnp.zeros_like(qprev_ref)
        kprev_ref[...] = jnp.zeros_like(kprev_ref)

    qx = q_ref[0].astype(F32)
    kx = k_ref[0].astype(F32)
    qc = _causal_conv_silu(qx, qprev_ref[...], wq_ref[...]) * (HEAD_DIM ** -0.5)
    kc = _causal_conv_silu(kx, kprev_ref[...], wk_ref[...])
    qprev_ref[...] = qx
    kprev_ref[...] = kx
    v = v_ref[0]

    li_r = gir_ref[...].reshape(1, ch)
    lf_r = _log_sigmoid(gfr_ref[...].reshape(1, ch))
    li_c = gic_ref[...].reshape(ch, 1)
    lf_c = _log_sigmoid(gfc_ref[...].reshape(ch, 1))

    row = lax.broadcasted_iota(jnp.int32, (ch, ch), 0)
    col = lax.broadcasted_iota(jnp.int32, (ch, ch), 1)
    tril = row >= col
    fc_c = jnp.sum(jnp.where(tril, lf_r, 0.0), axis=1, keepdims=True)
    fc_r = jnp.sum(jnp.where(row <= col, lf_c, 0.0), axis=0, keepdims=True)
    g = jnp.where(tril, fc_c - fc_r + li_r, NEG_BIG)
    m_prev = m_ref[...]
    inter = fc_c + m_prev
    m = jnp.maximum(jnp.max(g, axis=1, keepdims=True), inter)
    dmat = jnp.exp(g - m)
    a = jnp.exp(inter - m)

    qb = qc.astype(BF16)
    sc = _dot_nt(qb, kc.astype(BF16)) * dmat
    cst = c_ref[...]
    n_row = n_ref[...]
    num = _dot(sc.astype(BF16), v) + a * _dot(qb, cst.astype(BF16))
    den = jnp.sum(sc, axis=1, keepdims=True) + a * jnp.sum(qc * n_row, axis=1, keepdims=True)
    h = num / jnp.maximum(jnp.abs(den), jnp.exp(-m))

    f_last = fc_c[ch - 1:ch, :]
    logw_c = f_last - fc_c + li_c
    logw_r = f_last - fc_r + li_r
    m_new = jnp.maximum(f_last + m_prev, jnp.max(logw_r, axis=1, keepdims=True))
    dec = jnp.exp(f_last + m_prev - m_new)
    kw = kc * jnp.exp(logw_c - m_new)
    c_ref[...] = dec * cst + _dot_tn(kw.astype(BF16), v)
    n_ref[...] = dec * n_row + jnp.sum(kw, axis=0, keepdims=True)
    m_ref[...] = m_new

    y = h * lax.rsqrt(jnp.mean(h * h, axis=-1, keepdims=True) + EPS) * ng_ref[...]
    o_ref[0] = (y * _sigmoid(og_ref[0].astype(F32))).astype(o_ref.dtype)


def _mlstm(h_b, conv_w, gate_i, gate_f, norm_g):
    bsz, s_len, _ = h_b.shape
    ch = MLSTM_CHUNK
    nc = s_len // ch

    def col(j):
        return pl.BlockSpec((1, ch, HEAD_DIM), lambda b, h, c, j=j: (b, c, j * B_HEADS + h))

    def rows(gt):
        return gt.transpose(0, 2, 1).reshape(bsz, B_HEADS, nc, 1, ch)

    def cols(gt):
        return gt.transpose(0, 2, 1).reshape(bsz, B_HEADS, s_len, 1)

    row_spec = pl.BlockSpec((1, 1, 1, 1, ch), lambda b, h, c: (b, h, c, 0, 0))
    col_spec = pl.BlockSpec((1, 1, ch, 1), lambda b, h, c: (b, h, c, 0))
    return pl.pallas_call(
        _mlstm_kernel,
        grid=(bsz, B_HEADS, nc),
        in_specs=[col(0), col(1), col(2), col(3),
                  pl.BlockSpec((CONV_K, HEAD_DIM), lambda b, h, c: (0, h)),
                  pl.BlockSpec((CONV_K, HEAD_DIM), lambda b, h, c: (0, B_HEADS + h)),
                  row_spec, row_spec, col_spec, col_spec,
                  pl.BlockSpec((1, HEAD_DIM), lambda b, h, c: (0, 0))],
        out_specs=pl.BlockSpec((1, ch, HEAD_DIM), lambda b, h, c: (b, c, h)),
        out_shape=jax.ShapeDtypeStruct((bsz, s_len, B_HEADS * HEAD_DIM), BF16),
        scratch_shapes=[pltpu.VMEM((HEAD_DIM, HEAD_DIM), F32), pltpu.VMEM((1, HEAD_DIM), F32),
                        pltpu.VMEM((1, 1), F32), pltpu.VMEM((ch, HEAD_DIM), F32),
                        pltpu.VMEM((ch, HEAD_DIM), F32)],
        compiler_params=_cparams("parallel", "parallel", "arbitrary"),
        name="mlstm",
    )(h_b, h_b, h_b, h_b, conv_w, conv_w, rows(gate_i), rows(gate_f), cols(gate_i), cols(gate_f),
      norm_g.reshape(1, HEAD_DIM))


def _diff_attn_kernel(q_ref, k_ref, v_ref, cos_ref, sa_ref, sb_ref, qg_ref, kg_ref, lam_ref, sg_ref,
                      o_ref, q0_ref, q1_ref, kp_ref, *, lambda_init):
    s_len = q_ref.shape[1]
    blk = ATT_BLOCK
    nblk = s_len // blk
    lane = lax.broadcasted_iota(jnp.int32, (blk, LANES), 1)
    first_map = lane < C_DQK
    li = lax.broadcasted_iota(jnp.int32, (LANES, LANES), 0) // C_DQK
    lj = lax.broadcasted_iota(jnp.int32, (LANES, LANES), 1) // C_DQK
    same_map = jnp.where(li == lj, 1.0, 0.0).astype(BF16)

    def qk_norm_rope(x, g, rows):
        ss = _split_dot_rhs(x * x, same_map)
        y = x * lax.rsqrt(ss * (1.0 / C_DQK) + EPS) * g
        return (y * cos_ref[rows, :] + pltpu.roll(y, LANES - ROPE_HALF, 1) * sa_ref[rows, :]
                + pltpu.roll(y, ROPE_HALF, 1) * sb_ref[rows, :])

    def prep(i, carry):
        rows = pl.ds(pl.multiple_of(i * blk, blk), blk)
        qn = qk_norm_rope(q_ref[0, rows, :].astype(F32), qg_ref[...], rows) * (C_DQK ** -0.5)
        q0_ref[rows, :] = jnp.where(first_map, qn, 0.0).astype(BF16)
        q1_ref[rows, :] = jnp.where(first_map, 0.0, qn).astype(BF16)
        kp_ref[rows, :] = qk_norm_rope(k_ref[0, rows, :].astype(F32), kg_ref[...], rows).astype(BF16)
        return carry

    lax.fori_loop(0, nblk, prep, 0)

    lamv = lam_ref[...]
    lam = (jnp.exp(jnp.sum(lamv[0:1] * lamv[1:2], axis=1, keepdims=True))
           - jnp.exp(jnp.sum(lamv[2:3] * lamv[3:4], axis=1, keepdims=True)) + lambda_init)
    brow = lax.broadcasted_iota(jnp.int32, (blk, blk), 0) // MASK_CHUNK
    bcol = lax.broadcasted_iota(jnp.int32, (blk, blk), 1) // MASK_CHUNK
    diag_ok = brow >= bcol

    def one_map(qm, i):
        def update(carry, j, mask):
            m, l, acc = carry
            rows = pl.ds(pl.multiple_of(j * blk, blk), blk)
            s = _dot_nt(qm, kp_ref[rows, :])
            if mask:
                s = jnp.where(diag_ok, s, NEG_BIG)
            m_new = jnp.maximum(m, jnp.max(s, axis=1, keepdims=True))
            p = jnp.exp(s - m_new)
            alpha = jnp.exp(m - m_new)
            l = alpha * l + jnp.sum(p, axis=1, keepdims=True)
            acc = alpha * acc + _dot(p.astype(BF16), v_ref[0, rows, :])
            return m_new, l, acc

        init = (jnp.full((blk, 1), NEG_BIG, F32), jnp.zeros((blk, 1), F32), jnp.zeros((blk, HEAD_DIM), F32))
        carry = lax.fori_loop(0, i, lambda j, cr: update(cr, j, False), init)
        _, l, acc = update(carry, i, True)
        return acc / l

    def qblock(i, carry):
        rows = pl.ds(pl.multiple_of(i * blk, blk), blk)
        out = one_map(q0_ref[rows, :], i) - lam * one_map(q1_ref[rows, :], i)
        y = out * lax.rsqrt(jnp.mean(out * out, axis=-1, keepdims=True) + EPS) * sg_ref[...]
        o_ref[0, rows, :] = (y * (1.0 - lambda_init)).astype(o_ref.dtype)
        return carry

    lax.fori_loop(0, nblk, qblock, 0)


def _rope_tables(s_len):
    pos = jnp.arange(s_len, dtype=F32)
    inv_freq = ROPE_THETA ** (-jnp.arange(0, ROPE_DIM, 2, dtype=F32) / ROPE_DIM)
    ang = pos[:, None] * inv_freq[None, :]
    cos, sin = jnp.cos(ang), jnp.sin(ang)
    ones = jnp.ones((s_len, C_DQK - ROPE_DIM), F32)
    zeros = jnp.zeros((s_len, C_DQK - ROPE_DIM), F32)
    zh = jnp.zeros((s_len, ROPE_HALF), F32)
    cos_map = jnp.concatenate([cos, cos, ones], axis=1)
    sa_map = jnp.concatenate([-sin, zh, zeros], axis=1)
    sb_map = jnp.concatenate([zh, sin, zeros], axis=1)
    return tuple(jnp.concatenate([t, t], axis=1) for t in (cos_map, sa_map, sb_map))


def _diff_attn(h_c, tables, qn_g, kn_g, lam_vecs, subln_g, lambda_init):
    bsz, s_len, _ = h_c.shape

    def col(j):
        return pl.BlockSpec((1, s_len, HEAD_DIM), lambda b, h, j=j: (b, 0, j * C_HEADS + h))

    def full(shape):
        return pl.BlockSpec(shape, lambda b, h: (0,) * len(shape))

    return pl.pallas_call(
        functools.partial(_diff_attn_kernel, lambda_init=lambda_init),
        grid=(bsz, C_HEADS),
        in_specs=[col(0), col(1), col(2), full((s_len, LANES)), full((s_len, LANES)), full((s_len, LANES)),
                  full((1, LANES)), full((1, LANES)), full((4, C_DQK)), full((1, HEAD_DIM))],
        out_specs=pl.BlockSpec((1, s_len, HEAD_DIM), lambda b, h: (b, 0, h)),
        out_shape=jax.ShapeDtypeStruct((bsz, s_len, C_HEADS * HEAD_DIM), BF16),
        scratch_shapes=[pltpu.VMEM((s_len, LANES), BF16), pltpu.VMEM((s_len, LANES), BF16),
                        pltpu.VMEM((s_len, LANES), BF16)],
        compiler_params=_cparams("parallel", "parallel"),
        name="diff_attn",
    )(h_c, h_c, h_c, *tables, jnp.tile(qn_g, 2).reshape(1, LANES), jnp.tile(kn_g, 2).reshape(1, LANES),
      lam_vecs, subln_g.reshape(1, HEAD_DIM))


def _merge_kernel(oa_ref, ob_ref, oc_ref, wa_ref, wb_ref, wc_ref, ga_ref, gb_ref, gc_ref, y_ref):
    y = _sigmoid(ga_ref[...].astype(F32)) * _dot(oa_ref[...], wa_ref[...])
    y = y + _sigmoid(gb_ref[...].astype(F32)) * _dot(ob_ref[...], wb_ref[...])
    y = y + _sigmoid(gc_ref[...].astype(F32)) * _dot(oc_ref[...], wc_ref[...])
    y_ref[...] = y.astype(y_ref.dtype)


def _merge(oa, ob, oc, wa, wb, wc, gates):
    m = oa.shape[0]
    n = wa.shape[1]
    tm = _pick_tile(m, (512, 256, 128))
    tn = _pick_tile(n, (1024, 512, 256, 128))
    nb = n // tn

    def lhs(a):
        return pl.BlockSpec((tm, a.shape[1]), lambda j, i: (i, 0))

    def rhs(w):
        return pl.BlockSpec((w.shape[0], tn), lambda j, i: (0, j))

    def gate(k):
        return pl.BlockSpec((tm, tn), lambda j, i, k=k: (i, k * nb + j))

    return pl.pallas_call(
        _merge_kernel,
        grid=(nb, m // tm),
        in_specs=[lhs(oa), lhs(ob), lhs(oc), rhs(wa), rhs(wb), rhs(wc), gate(0), gate(1), gate(2)],
        out_specs=pl.BlockSpec((tm, tn), lambda j, i: (i, j)),
        out_shape=jax.ShapeDtypeStruct((m, n), BF16),
        compiler_params=_cparams("parallel", "parallel"),
        name="merge",
    )(oa, ob, oc, wa, wb, wc, gates, gates, gates)


def _router_kernel(x_ref, g_ref, w_ref, b_ref, xn_ref, route_ref):
    x = x_ref[...]
    xn = x * lax.rsqrt(jnp.mean(x * x, axis=-1, keepdims=True) + EPS) * g_ref[...]
    xn_ref[...] = xn
    logits = jnp.dot(xn, w_ref[...], preferred_element_type=F32, precision=lax.Precision.HIGHEST) + b_ref[...]
    lane = lax.broadcasted_iota(jnp.int32, logits.shape, 1).astype(F32)
    none = float(LANES)

    def first_argmax(vals):
        top = jnp.max(vals, axis=1, keepdims=True)
        return top, jnp.min(jnp.where(vals == top, lane, none), axis=1, keepdims=True)

    is_group = lane < N_GROUPS
    g_top, g_idx = first_argmax(jnp.where(is_group, logits, NEG_BIG))
    g_w = 1.0 / jnp.sum(jnp.where(is_group, jnp.exp(logits - g_top), 0.0), axis=1, keepdims=True)
    lo = N_GROUPS + EXPERTS_PER_GROUP * g_idx
    e_logits = jnp.where((lane >= lo) & (lane < lo + EXPERTS_PER_GROUP), logits, NEG_BIG)
    v1, i1 = first_argmax(e_logits)
    v2, i2 = first_argmax(jnp.where(lane == i1, NEG_BIG, e_logits))
    e2 = jnp.exp(v2 - v1)
    w1 = g_w / (1.0 + e2)
    w2 = w1 * e2
    route = jnp.where(lane == 0, i1 - N_GROUPS, jnp.where(lane == 1, i2 - N_GROUPS,
                      jnp.where(lane == 2, w1, jnp.where(lane == 3, w2, 0.0))))
    route_ref[...] = route


def _router(x, g, w_router, b_router, tm=256):
    t, d = x.shape
    return pl.pallas_call(
        _router_kernel,
        grid=(t // tm,),
        in_specs=[pl.BlockSpec((tm, d), lambda i: (i, 0)), pl.BlockSpec((1, d), lambda i: (0, 0)),
                  pl.BlockSpec((d, LANES), lambda i: (0, 0)), pl.BlockSpec((1, LANES), lambda i: (0, 0))],
        out_specs=[pl.BlockSpec((tm, d), lambda i: (i, 0)), pl.BlockSpec((tm, LANES), lambda i: (i, 0))],
        out_shape=[jax.ShapeDtypeStruct((t, d), F32), jax.ShapeDtypeStruct((t, LANES), F32)],
        compiler_params=_cparams("parallel"),
        name="router",
    )(x, g.reshape(1, d), w_router, b_router)


def _gather_rows(src_hbm, dst_ref, idx_ref, base, n_rows, sem):
    def body(r, carry):
        pltpu.make_async_copy(src_hbm.at[pl.ds(idx_ref[base + r], 1)], dst_ref.at[pl.ds(r, 1)], sem).start()
        return carry
    lax.fori_loop(0, n_rows, body, 0)


def _wait_rows(src_hbm, dst_ref, n_rows, sem):
    pltpu.make_async_copy(src_hbm.at[pl.ds(0, n_rows)], dst_ref, sem).wait()


def _expert_kernel(tile_expert_ref, row_token_ref, x_hbm, w1_ref, w3_ref, w2_ref, rw_ref, y_ref, xbuf, sem):
    del tile_expert_ref
    i = pl.program_id(0)
    n = pl.num_programs(0)
    tm = MOE_TILE
    slot = i % 2

    @pl.when(i == 0)
    def _():
        _gather_rows(x_hbm, xbuf.at[0], row_token_ref, 0, tm, sem.at[0])

    @pl.when(i + 1 < n)
    def _():
        _gather_rows(x_hbm, xbuf.at[1 - slot], row_token_ref, (i + 1) * tm, tm, sem.at[1 - slot])

    _wait_rows(x_hbm, xbuf.at[slot], tm, sem.at[slot])
    x = xbuf[slot].astype(BF16)
    h1 = _dot(x, w1_ref[0])
    h3 = _dot(x, w3_ref[0])
    h = (h1 * _sigmoid(h1) * h3 * rw_ref[0]).astype(BF16)
    y_ref[...] = _dot(h, w2_ref[0])


def _experts(xn, w1, w3, w2, tile_expert, row_token, row_weight):
    t, d = xn.shape
    ff = w1.shape[2]
    tm = MOE_TILE
    n_tiles = tile_expert.shape[0]
    grid_spec = pltpu.PrefetchScalarGridSpec(
        num_scalar_prefetch=2,
        grid=(n_tiles,),
        in_specs=[pl.BlockSpec(memory_space=pl.ANY),
                  pl.BlockSpec((1, d, ff), lambda i, te, rt: (te[i], 0, 0)),
                  pl.BlockSpec((1, d, ff), lambda i, te, rt: (te[i], 0, 0)),
                  pl.BlockSpec((1, ff, d), lambda i, te, rt: (te[i], 0, 0)),
                  pl.BlockSpec((1, tm, 1), lambda i, te, rt: (i, 0, 0))],
        out_specs=pl.BlockSpec((tm, d), lambda i, te, rt: (i, 0)),
        scratch_shapes=[pltpu.VMEM((2, tm, d), F32), pltpu.SemaphoreType.DMA((2,))],
    )
    return pl.pallas_call(
        _expert_kernel,
        grid_spec=grid_spec,
        out_shape=jax.ShapeDtypeStruct((n_tiles * tm, d), F32),
        compiler_params=_cparams("arbitrary"),
        name="experts",
    )(tile_expert, row_token, xn, w1, w3, w2, row_weight.reshape(n_tiles, tm, 1))


def _combine_kernel(pos_ref, y_hbm, x_ref, g_ref, xo_ref, xn_ref, ybuf, sem):
    i = pl.program_id(0)
    n = pl.num_programs(0)
    tc = COMBINE_TILE
    slot = i % 2

    @pl.when(i == 0)
    def _():
        _gather_rows(y_hbm, ybuf.at[0], pos_ref, 0, 2 * tc, sem.at[0])

    @pl.when(i + 1 < n)
    def _():
        _gather_rows(y_hbm, ybuf.at[1 - slot], pos_ref, (i + 1) * 2 * tc, 2 * tc, sem.at[1 - slot])

    _wait_rows(y_hbm, ybuf.at[slot], 2 * tc, sem.at[slot])
    x = x_ref[...] + ybuf[slot, 0:tc, :] + ybuf[slot, tc:2 * tc, :]
    xo_ref[...] = x
    y = x * lax.rsqrt(jnp.mean(x * x, axis=-1, keepdims=True) + EPS) * g_ref[...]
    xn_ref[...] = y.astype(xn_ref.dtype)


def _combine(y_sorted, x, pos_tiles, next_g):
    t, d = x.shape
    tc = COMBINE_TILE
    grid_spec = pltpu.PrefetchScalarGridSpec(
        num_scalar_prefetch=1,
        grid=(t // tc,),
        in_specs=[pl.BlockSpec(memory_space=pl.ANY),
                  pl.BlockSpec((tc, d), lambda i, p: (i, 0)),
                  pl.BlockSpec((1, d), lambda i, p: (0, 0))],
        out_specs=[pl.BlockSpec((tc, d), lambda i, p: (i, 0)), pl.BlockSpec((tc, d), lambda i, p: (i, 0))],
        scratch_shapes=[pltpu.VMEM((2, 2 * tc, d), F32), pltpu.SemaphoreType.DMA((2,))],
    )
    return pl.pallas_call(
        _combine_kernel,
        grid_spec=grid_spec,
        out_shape=[jax.ShapeDtypeStruct((t, d), F32), jax.ShapeDtypeStruct((t, d), BF16)],
        compiler_params=_cparams("arbitrary"),
        name="combine",
    )(pos_tiles, y_sorted, x, next_g.reshape(1, d))


def _route_plan(expert_ids, weights, n_tokens):
    tm, tc = MOE_TILE, COMBINE_TILE
    n_assign = 2 * n_tokens
    n_tiles = n_assign // tm + N_EXPERTS
    e_flat = expert_ids.reshape(-1)
    onehot = (e_flat[:, None] == jnp.arange(N_EXPERTS, dtype=jnp.int32)[None, :]).astype(jnp.int32)
    csum = jnp.cumsum(onehot, axis=0)
    rank = jnp.sum(csum * onehot, axis=1) - 1
    counts = csum[-1]
    padded = ((counts + tm - 1) // tm) * tm
    ends = jnp.cumsum(padded)
    offs = ends - padded
    pos = jnp.sum(onehot * offs[None, :], axis=1) + rank
    row_token = jnp.zeros((n_tiles * tm,), jnp.int32).at[pos].set(jnp.arange(n_assign, dtype=jnp.int32) // 2)
    row_weight = jnp.zeros((n_tiles * tm,), F32).at[pos].set(weights.reshape(-1))
    tile_start = jnp.arange(n_tiles, dtype=jnp.int32) * tm
    tile_expert = jnp.minimum(jnp.sum((ends[None, :] <= tile_start[:, None]).astype(jnp.int32), axis=1),
                              N_EXPERTS - 1)
    pos_tiles = pos.reshape(n_tokens // tc, tc, 2).transpose(0, 2, 1).reshape(-1)
    return tile_expert, row_token, row_weight, pos_tiles


def kernel(x, norm1_g, w_in, hgrn_lb, hgrn_norm_g, mlstm_conv_w, mlstm_i_bias, mlstm_f_bias, mlstm_norm_g,
           diff_qn_g, diff_kn_g, diff_lam_q1, diff_lam_k1, diff_lam_q2, diff_lam_k2, diff_subln_g, w_branch_a,
           w_branch_b, w_branch_c, w_out, norm2_g, router_group_w, router_group_b, router_expert_w,
           router_expert_b, expert_w1, expert_w3, expert_w2):
    bsz, s_len, d = x.shape
    depth = w_in.shape[0]
    t = bsz * s_len
    a_w, b_w, c_w = A_HEADS * HEAD_DIM, B_HEADS * HEAD_DIM, C_HEADS * HEAD_DIM
    o_b = 4 * a_w
    o_s = o_b + 4 * b_w
    o_c = o_s + 2 * B_HEADS
    o_g = o_c + 3 * c_w

    tables = _rope_tables(s_len)
    p_lb = jax.nn.softmax(hgrn_lb.astype(F32), axis=0)
    lb_all = jnp.cumsum(p_lb, axis=0) - p_lb[0:1]

    xf = x.reshape(t, d)
    xn = _rmsnorm(xf, norm1_g[0], BF16)
    for l in range(depth):
        wl = w_in[l]
        h_a = _matmul(xn, wl[:, :o_b].astype(BF16), BF16, name="in_proj_a").reshape(bsz, s_len, -1)
        h_b = _matmul(xn, wl[:, o_b:o_s].astype(BF16), BF16, name="in_proj_b").reshape(bsz, s_len, -1)
        h_c = _matmul(xn, wl[:, o_c:o_g].astype(BF16), BF16, name="in_proj_c").reshape(bsz, s_len, -1)
        gates = _matmul(xn, wl[:, o_g:].astype(BF16), BF16, name="in_proj_gates")
        w_small = jnp.pad(wl[:, o_s:o_c], ((0, 0), (0, LANES - 2 * B_HEADS))).astype(BF16)
        h_s = _matmul(xn, w_small, F32, name="in_proj_scalar_gates").reshape(bsz, s_len, LANES)
        gate_i = h_s[..., :B_HEADS] + mlstm_i_bias[l]
        gate_f = h_s[..., B_HEADS:2 * B_HEADS] + mlstm_f_bias[l]

        out_a = _hgrn2(h_a, lb_all[l], hgrn_norm_g[l])
        out_b = _mlstm(h_b, mlstm_conv_w[l], gate_i, gate_f, mlstm_norm_g[l])
        lambda_init = 0.8 - 0.6 * math.exp(-0.3 * l)
        lam_vecs = jnp.stack([diff_lam_q1[l], diff_lam_k1[l], diff_lam_q2[l], diff_lam_k2[l]]).astype(F32)
        out_c = _diff_attn(h_c, tables, diff_qn_g[l], diff_kn_g[l], lam_vecs, diff_subln_g[l], lambda_init)

        y = _merge(out_a.reshape(t, a_w), out_b.reshape(t, b_w), out_c.reshape(t, c_w),
                   w_branch_a[l].astype(BF16), w_branch_b[l].astype(BF16), w_branch_c[l].astype(BF16), gates)
        xf = _matmul(y, w_out[l].astype(BF16), F32, residual=xf, name="out_proj")

        n_router = N_GROUPS + N_EXPERTS
        w_router = jnp.pad(jnp.concatenate([router_group_w[l], router_expert_w[l]], axis=1),
                           ((0, 0), (0, LANES - n_router)))
        b_router = jnp.pad(jnp.concatenate([router_group_b[l], router_expert_b[l]]),
                           (0, LANES - n_router)).reshape(1, LANES)
        xn2, route = _router(xf, norm2_g[l], w_router, b_router)
        tile_expert, row_token, row_weight, pos_tiles = _route_plan(
            route[:, 0:2].astype(jnp.int32), route[:, 2:4], t)
        ff = expert_w1.shape[-1]
        y_sorted = _experts(xn2, expert_w1[l].reshape(N_EXPERTS, d, ff).astype(BF16),
                            expert_w3[l].reshape(N_EXPERTS, d, ff).astype(BF16),
                            expert_w2[l].reshape(N_EXPERTS, ff, d).astype(BF16),
                            tile_expert, row_token, row_weight)
        next_g = norm1_g[l + 1] if l + 1 < depth else norm1_g[l]
        xf, xn = _combine(y_sorted, xf, pos_tiles, next_g)
    return xf.reshape(bsz, s_len, d)
```

```python
import functools
import math

import jax
import jax.numpy as jnp
from jax import lax
from jax.experimental import pallas as pl
from jax.experimental.pallas import tpu as pltpu

F32 = jnp.float32
BF16 = jnp.bfloat16

HEAD_DIM = 128
A_HEADS = 8
B_HEADS = 12
C_HEADS = 12
C_DQK = 64
CONV_K = 4
ROPE_THETA = 500000.0
ROPE_DIM = C_DQK // 4
ROPE_HALF = ROPE_DIM // 2
N_GROUPS = 4
EXPERTS_PER_GROUP = 8
N_EXPERTS = N_GROUPS * EXPERTS_PER_GROUP
EPS = 1e-6
LANES = 128
NEG_BIG = -1e30

VMEM_LIMIT = 56 * 1024 * 1024

HGRN_CHUNK = 128
HGRN_SUB = 32
HGRN_EXP_CLAMP = 60.0
MLSTM_CHUNK = 128
ATT_BLOCK = 256
MASK_CHUNK = 64
MOE_TILE = 256
COMBINE_TILE = 128


def _cparams(*sem):
    return pltpu.CompilerParams(dimension_semantics=sem, vmem_limit_bytes=VMEM_LIMIT)


def _sigmoid(x):
    return 1.0 / (1.0 + jnp.exp(-x))


def _dot(a, b):
    return jnp.dot(a, b, preferred_element_type=F32)


def _dot_nt(a, b):
    return lax.dot_general(a, b, (((1,), (1,)), ((), ())), preferred_element_type=F32)


def _dot_tn(a, b):
    return lax.dot_general(a, b, (((0,), (0,)), ((), ())), preferred_element_type=F32)


def _split_dot(exact_lhs, x):
    hi = x.astype(BF16)
    lo = (x - hi.astype(F32)).astype(BF16)
    return _dot(exact_lhs, hi) + _dot(exact_lhs, lo)


def _split_dot_rhs(x, exact_rhs):
    hi = x.astype(BF16)
    lo = (x - hi.astype(F32)).astype(BF16)
    return _dot(hi, exact_rhs) + _dot(lo, exact_rhs)


def _rmsnorm_kernel(x_ref, g_ref, o_ref):
    x = x_ref[...]
    y = x * lax.rsqrt(jnp.mean(x * x, axis=-1, keepdims=True) + EPS) * g_ref[...]
    o_ref[...] = y.astype(o_ref.dtype)


def _rmsnorm(x, g, out_dtype, tm=256):
    t, d = x.shape
    return pl.pallas_call(
        _rmsnorm_kernel,
        grid=(t // tm,),
        in_specs=[pl.BlockSpec((tm, d), lambda i: (i, 0)), pl.BlockSpec((1, d), lambda i: (0, 0))],
        out_specs=pl.BlockSpec((tm, d), lambda i: (i, 0)),
        out_shape=jax.ShapeDtypeStruct((t, d), out_dtype),
        compiler_params=_cparams("parallel"),
        name="rmsnorm",
    )(x, g.reshape(1, d))


def _mm_kernel(x_ref, w_ref, o_ref):
    o_ref[...] = _dot(x_ref[...], w_ref[...]).astype(o_ref.dtype)


def _mm_res_kernel(x_ref, w_ref, r_ref, o_ref):
    o_ref[...] = (r_ref[...] + _dot(x_ref[...], w_ref[...])).astype(o_ref.dtype)


def _pick_tile(n, candidates):
    for c in candidates:
        if n % c == 0:
            return c
    return n


def _matmul(x, w, out_dtype, residual=None, name="matmul"):
    m, k = x.shape
    n = w.shape[1]
    tm = _pick_tile(m, (512, 256, 128))
    tn = _pick_tile(n, (1024, 768, 512, 256, 128))
    in_specs = [pl.BlockSpec((tm, k), lambda j, i: (i, 0)), pl.BlockSpec((k, tn), lambda j, i: (0, j))]
    args = [x, w]
    kern = _mm_kernel
    if residual is not None:
        in_specs.append(pl.BlockSpec((tm, tn), lambda j, i: (i, j)))
        args.append(residual)
        kern = _mm_res_kernel
    return pl.pallas_call(
        kern,
        grid=(n // tn, m // tm),
        in_specs=in_specs,
        out_specs=pl.BlockSpec((tm, tn), lambda j, i: (i, j)),
        out_shape=jax.ShapeDtypeStruct((m, n), out_dtype),
        compiler_params=_cparams("parallel", "parallel"),
        name=name,
    )(*args)


def _hgrn2_head(qp, z, v, gp, lb, ng, st_ref):
    ch, sb = HGRN_CHUNK, HGRN_SUB
    nsb = ch // sb
    sig = _sigmoid(z)
    logf = jnp.log(lb + (1.0 - lb) * sig)
    k = (1.0 - lb) * (1.0 - sig)
    q = qp * _sigmoid(qp)

    row = lax.broadcasted_iota(jnp.int32, (ch, ch), 0)
    col = lax.broadcasted_iota(jnp.int32, (ch, ch), 1)
    causal = row >= col
    tri = jnp.where(causal, 1.0, 0.0).astype(BF16)
    b = _split_dot(tri, logf)
    b_last = b[ch - 1:ch, :]

    st = st_ref[...]
    o = _dot_nt((q * jnp.exp(b)).astype(BF16), st.astype(BF16))

    refs = [b[i * sb:i * sb + 1, :] for i in range(nsb)]
    rblk = jnp.concatenate([jnp.broadcast_to(r, (sb, HEAD_DIM)) for r in refs], axis=0)
    qt = q * jnp.exp(b - rblk)
    rowblk = lax.broadcasted_iota(jnp.int32, (ch, HEAD_DIM), 0) // sb
    qcat = jnp.concatenate([jnp.where(rowblk == i, qt, 0.0).astype(BF16) for i in range(nsb)], axis=1)
    kcat = jnp.concatenate(
        [(k * jnp.exp(jnp.minimum(r - b, HGRN_EXP_CLAMP))).astype(BF16) for r in refs], axis=1)
    s = jnp.where(causal, _dot_nt(qcat, kcat), 0.0)
    o = o + _dot(s.astype(BF16), v)

    kd = (k * jnp.exp(b_last - b)).astype(BF16)
    st_ref[...] = st * jnp.exp(b_last) + _dot_tn(v, kd)

    y = o * lax.rsqrt(jnp.mean(o * o, axis=-1, keepdims=True) + EPS) * ng
    return y * (gp * _sigmoid(gp))


def _hgrn2_kernel(q_ref, f_ref, i_ref, g_ref, lb_ref, ng_ref, o_ref, st_ref):
    @pl.when(pl.program_id(1) == 0)
    def _():
        st_ref[...] = jnp.zeros_like(st_ref)

    for h in range(A_HEADS):
        cols = slice(h * HEAD_DIM, (h + 1) * HEAD_DIM)
        y = _hgrn2_head(q_ref[0, :, cols].astype(F32), f_ref[0, :, cols].astype(F32), i_ref[0, :, cols],
                        g_ref[0, :, cols].astype(F32), lb_ref[:, cols], ng_ref[...], st_ref.at[h])
        o_ref[0, :, cols] = y.astype(o_ref.dtype)


def _hgrn2(h_a, lb, norm_g):
    bsz, s_len, _ = h_a.shape
    ch = HGRN_CHUNK
    a_w = A_HEADS * HEAD_DIM

    def col(j):
        return pl.BlockSpec((1, ch, a_w), lambda b, c, j=j: (b, c, j))

    return pl.pallas_call(
        _hgrn2_kernel,
        grid=(bsz, s_len // ch),
        in_specs=[col(0), col(1), col(2), col(3),
                  pl.BlockSpec((1, a_w), lambda b, c: (0, 0)),
                  pl.BlockSpec((1, HEAD_DIM), lambda b, c: (0, 0))],
        out_specs=pl.BlockSpec((1, ch, a_w), lambda b, c: (b, c, 0)),
        out_shape=jax.ShapeDtypeStruct((bsz, s_len, a_w), BF16),
        scratch_shapes=[pltpu.VMEM((A_HEADS, HEAD_DIM, HEAD_DIM), F32)],
        compiler_params=_cparams("parallel", "arbitrary"),
        name="hgrn2",
    )(h_a, h_a, h_a, h_a, lb.reshape(1, -1), norm_g.reshape(1, HEAD_DIM))


def _log_sigmoid(x):
    return jnp.minimum(x, 0.0) - jnp.log(1.0 + jnp.exp(-jnp.abs(x)))


def _causal_conv_silu(x, prev, w):
    rowi = lax.broadcasted_iota(jnp.int32, x.shape, 0)
    acc = x * w[CONV_K - 1:CONV_K, :]
    for sft in range(1, CONV_K):
        shifted = jnp.where(rowi >= sft, pltpu.roll(x, sft, 0), pltpu.roll(prev, sft, 0))
        acc = acc + shifted * w[CONV_K - 1 - sft:CONV_K - sft, :]
    return acc * _sigmoid(acc)


def _mlstm_head(qc, kc, v, og, li_r, gf_r, li_c, gf_c, ng, c_ref, n_ref, m_ref):
    ch = MLSTM_CHUNK
    lf_r = _log_sigmoid(gf_r)
    lf_c = _log_sigmoid(gf_c)
    row = lax.broadcasted_iota(jnp.int32, (ch, ch), 0)
    col = lax.broadcasted_iota(jnp.int32, (ch, ch), 1)
    tril = row >= col
    fc_c = jnp.sum(jnp.where(tril, lf_r, 0.0), axis=1, keepdims=True)
    fc_r = jnp.sum(jnp.where(row <= col, lf_c, 0.0), axis=0, keepdims=True)
    g = jnp.where(tril, fc_c - fc_r + li_r, NEG_BIG)
    m_prev = m_ref[...]
    inter = fc_c + m_prev
    m = jnp.maximum(jnp.max(g, axis=1, keepdims=True), inter)
    dmat = jnp.exp(g - m)
    a = jnp.exp(inter - m)

    qb = qc.astype(BF16)
    sc = _dot_nt(qb, kc.astype(BF16)) * dmat
    cst = c_ref[...]
    n_row = n_ref[...]
    num = _dot(sc.astype(BF16), v) + a * _dot(qb, cst.astype(BF16))
    den = jnp.sum(sc, axis=1, keepdims=True) + a * jnp.sum(qc * n_row, axis=1, keepdims=True)
    h = num / jnp.maximum(jnp.abs(den), jnp.exp(-m))

    f_last = fc_c[ch - 1:ch, :]
    logw_c = f_last - fc_c + li_c
    logw_r = f_last - fc_r + li_r
    m_new = jnp.maximum(f_last + m_prev, jnp.max(logw_r, axis=1, keepdims=True))
    dec = jnp.exp(f_last + m_prev - m_new)
    kw = kc * jnp.exp(logw_c - m_new)
    c_ref[...] = dec * cst + _dot_tn(kw.astype(BF16), v)
    n_ref[...] = dec * n_row + jnp.sum(kw, axis=0, keepdims=True)
    m_ref[...] = m_new

    y = h * lax.rsqrt(jnp.mean(h * h, axis=-1, keepdims=True) + EPS) * ng
    return y * _sigmoid(og)


def _mlstm_kernel(q_ref, k_ref, v_ref, og_ref, w_ref, gir_ref, gfr_ref, gic_ref, gfc_ref, ng_ref,
                  o_ref, c_ref, n_ref, m_ref, qprev_ref, kprev_ref):
    @pl.when(pl.program_id(1) == 0)
    def _():
        c_ref[...] = jnp.zeros_like(c_ref)
        n_ref[...] = jnp.zeros_like(n_ref)
        m_ref[...] = jnp.zeros_like(m_ref)
        qprev_ref[...] = jnp.zeros_like(qprev_ref)
        kprev_ref[...] = jnp.zeros_like(kprev_ref)

    b_w = B_HEADS * HEAD_DIM
    for h in range(B_HEADS):
        cols = slice(h * HEAD_DIM, (h + 1) * HEAD_DIM)
        kcols = slice(b_w + h * HEAD_DIM, b_w + (h + 1) * HEAD_DIM)
        qx = q_ref[0, :, cols].astype(F32)
        kx = k_ref[0, :, cols].astype(F32)
        qc = _causal_conv_silu(qx, qprev_ref[:, cols], w_ref[:, cols]) * (HEAD_DIM ** -0.5)
        kc = _causal_conv_silu(kx, kprev_ref[:, cols], w_ref[:, kcols])
        qprev_ref[:, cols] = qx
        kprev_ref[:, cols] = kx
        y = _mlstm_head(qc, kc, v_ref[0, :, cols], og_ref[0, :, cols].astype(F32),
                        gir_ref[0, 0, h:h + 1, :], gfr_ref[0, 0, h:h + 1, :],
                        gic_ref[0, :, h:h + 1], gfc_ref[0, :, h:h + 1], ng_ref[...],
                        c_ref.at[h], n_ref.at[h], m_ref.at[h])
        o_ref[0, :, cols] = y.astype(o_ref.dtype)


def _mlstm(h_b, conv_w, gate_i, gate_f, norm_g):
    bsz, s_len, _ = h_b.shape
    ch = MLSTM_CHUNK
    nc = s_len // ch
    b_w = B_HEADS * HEAD_DIM

    def col(j):
        return pl.BlockSpec((1, ch, b_w), lambda b, c, j=j: (b, c, j))

    def rows(gt):
        return gt.reshape(bsz, nc, ch, B_HEADS).transpose(0, 1, 3, 2)

    row_spec = pl.BlockSpec((1, 1, B_HEADS, ch), lambda b, c: (b, c, 0, 0))
    col_spec = pl.BlockSpec((1, ch, B_HEADS), lambda b, c: (b, c, 0))
    return pl.pallas_call(
        _mlstm_kernel,
        grid=(bsz, nc),
        in_specs=[col(0), col(1), col(2), col(3),
                  pl.BlockSpec((CONV_K, 2 * b_w), lambda b, c: (0, 0)),
                  row_spec, row_spec, col_spec, col_spec,
                  pl.BlockSpec((1, HEAD_DIM), lambda b, c: (0, 0))],
        out_specs=pl.BlockSpec((1, ch, b_w), lambda b, c: (b, c, 0)),
        out_shape=jax.ShapeDtypeStruct((bsz, s_len, b_w), BF16),
        scratch_shapes=[pltpu.VMEM((B_HEADS, HEAD_DIM, HEAD_DIM), F32), pltpu.VMEM((B_HEADS, 1, HEAD_DIM), F32),
                        pltpu.VMEM((B_HEADS, 1, 1), F32), pltpu.VMEM((ch, b_w), F32),
                        pltpu.VMEM((ch, b_w), F32)],
        compiler_params=_cparams("parallel", "arbitrary"),
        name="mlstm",
    )(h_b, h_b, h_b, h_b, conv_w, rows(gate_i), rows(gate_f), gate_i, gate_f, norm_g.reshape(1, HEAD_DIM))


def _diff_attn_kernel(q_ref, k_ref, v_ref, cos_ref, sa_ref, sb_ref, qg_ref, kg_ref, lam_ref, sg_ref,
                      o_ref, qs_ref, kp_ref, vx_ref, *, lambda_init):
    s_len = q_ref.shape[1]
    tq = ATT_BLOCK
    nq = s_len // tq
    lane = lax.broadcasted_iota(jnp.int32, (tq, LANES), 1)
    first_map = lane < C_DQK
    li = lax.broadcasted_iota(jnp.int32, (LANES, LANES), 0) // C_DQK
    lj = lax.broadcasted_iota(jnp.int32, (LANES, LANES), 1) // C_DQK
    same_map = jnp.where(li == lj, 1.0, 0.0).astype(BF16)

    def qk_norm_rope(x, g, rows):
        ss = _split_dot_rhs(x * x, same_map)
        y = x * lax.rsqrt(ss * (1.0 / C_DQK) + EPS) * g
        return (y * cos_ref[rows, :] + pltpu.roll(y, LANES - ROPE_HALF, 1) * sa_ref[rows, :]
                + pltpu.roll(y, ROPE_HALF, 1) * sb_ref[rows, :])

    def prep(i, carry):
        rows = pl.ds(pl.multiple_of(i * tq, tq), tq)
        qn = qk_norm_rope(q_ref[0, rows, :].astype(F32), qg_ref[...], rows) * (C_DQK ** -0.5)
        qs_ref[i, 0:tq, :] = jnp.where(first_map, qn, 0.0).astype(BF16)
        qs_ref[i, tq:2 * tq, :] = jnp.where(first_map, 0.0, qn).astype(BF16)
        kp_ref[rows, :] = qk_norm_rope(k_ref[0, rows, :].astype(F32), kg_ref[...], rows).astype(BF16)
        vx_ref[rows, 0:HEAD_DIM] = v_ref[0, rows, :]
        vx_ref[rows, HEAD_DIM:2 * HEAD_DIM] = jnp.ones((tq, HEAD_DIM), BF16)
        return carry

    lax.fori_loop(0, nq, prep, 0)

    lamv = lam_ref[...]
    lam = (jnp.exp(jnp.sum(lamv[0:1] * lamv[1:2], axis=1, keepdims=True))
           - jnp.exp(jnp.sum(lamv[2:3] * lamv[3:4], axis=1, keepdims=True)) + lambda_init)
    qchunk = (lax.broadcasted_iota(jnp.int32, (2 * tq, tq), 0) % tq) // MASK_CHUNK
    kchunk = lax.broadcasted_iota(jnp.int32, (2 * tq, tq), 1) // MASK_CHUNK
    diag_ok = qchunk >= kchunk

    for i in range(nq):
        k0 = i * tq
        qs = qs_ref[i]
        s_d = jnp.where(diag_ok, _dot_nt(qs, kp_ref[k0:k0 + tq, :]), NEG_BIG)
        m = jnp.max(s_d, axis=1, keepdims=True)
        if i > 0:
            s_o = _dot_nt(qs, kp_ref[0:k0, :])
            m = jnp.maximum(m, jnp.max(s_o, axis=1, keepdims=True))
            acc = _dot(jnp.exp(s_o - m).astype(BF16), vx_ref[0:k0, :])
            acc = acc + _dot(jnp.exp(s_d - m).astype(BF16), vx_ref[k0:k0 + tq, :])
        else:
            acc = _dot(jnp.exp(s_d - m).astype(BF16), vx_ref[k0:k0 + tq, :])
        o = acc[:, 0:HEAD_DIM] / acc[:, HEAD_DIM:2 * HEAD_DIM]
        out = o[0:tq] - lam * o[tq:2 * tq]
        y = out * lax.rsqrt(jnp.mean(out * out, axis=-1, keepdims=True) + EPS) * sg_ref[...]
        o_ref[0, k0:k0 + tq, :] = (y * (1.0 - lambda_init)).astype(o_ref.dtype)


def _rope_tables(s_len):
    pos = jnp.arange(s_len, dtype=F32)
    inv_freq = ROPE_THETA ** (-jnp.arange(0, ROPE_DIM, 2, dtype=F32) / ROPE_DIM)
    ang = pos[:, None] * inv_freq[None, :]
    cos, sin = jnp.cos(ang), jnp.sin(ang)
    ones = jnp.ones((s_len, C_DQK - ROPE_DIM), F32)
    zeros = jnp.zeros((s_len, C_DQK - ROPE_DIM), F32)
    zh = jnp.zeros((s_len, ROPE_HALF), F32)
    cos_map = jnp.concatenate([cos, cos, ones], axis=1)
    sa_map = jnp.concatenate([-sin, zh, zeros], axis=1)
    sb_map = jnp.concatenate([zh, sin, zeros], axis=1)
    return tuple(jnp.concatenate([t, t], axis=1) for t in (cos_map, sa_map, sb_map))


def _diff_attn(h_c, tables, qn_g, kn_g, lam_vecs, subln_g, lambda_init):
    bsz, s_len, _ = h_c.shape

    def col(j):
        return pl.BlockSpec((1, s_len, HEAD_DIM), lambda b, h, j=j: (b, 0, j * C_HEADS + h))

    def full(shape):
        return pl.BlockSpec(shape, lambda b, h: (0,) * len(shape))

    return pl.pallas_call(
        functools.partial(_diff_attn_kernel, lambda_init=lambda_init),
        grid=(bsz, C_HEADS),
        in_specs=[col(0), col(1), col(2), full((s_len, LANES)), full((s_len, LANES)), full((s_len, LANES)),
                  full((1, LANES)), full((1, LANES)), full((4, C_DQK)), full((1, HEAD_DIM))],
        out_specs=pl.BlockSpec((1, s_len, HEAD_DIM), lambda b, h: (b, 0, h)),
        out_shape=jax.ShapeDtypeStruct((bsz, s_len, C_HEADS * HEAD_DIM), BF16),
        scratch_shapes=[pltpu.VMEM((s_len // ATT_BLOCK, 2 * ATT_BLOCK, LANES), BF16),
                        pltpu.VMEM((s_len, LANES), BF16), pltpu.VMEM((s_len, 2 * HEAD_DIM), BF16)],
        compiler_params=_cparams("parallel", "parallel"),
        name="diff_attn",
    )(h_c, h_c, h_c, *tables, jnp.tile(qn_g, 2).reshape(1, LANES), jnp.tile(kn_g, 2).reshape(1, LANES),
      lam_vecs, subln_g.reshape(1, HEAD_DIM))


def _merge_kernel(oa_ref, ob_ref, oc_ref, wa_ref, wb_ref, wc_ref, ga_ref, gb_ref, gc_ref, y_ref):
    y = _sigmoid(ga_ref[...].astype(F32)) * _dot(oa_ref[...], wa_ref[...])
    y = y + _sigmoid(gb_ref[...].astype(F32)) * _dot(ob_ref[...], wb_ref[...])
    y = y + _sigmoid(gc_ref[...].astype(F32)) * _dot(oc_ref[...], wc_ref[...])
    y_ref[...] = y.astype(y_ref.dtype)


def _merge(oa, ob, oc, wa, wb, wc, gates):
    m = oa.shape[0]
    n = wa.shape[1]
    tm = _pick_tile(m, (512, 256, 128))
    tn = _pick_tile(n, (1024, 512, 256, 128))
    nb = n // tn

    def lhs(a):
        return pl.BlockSpec((tm, a.shape[1]), lambda j, i: (i, 0))

    def rhs(w):
        return pl.BlockSpec((w.shape[0], tn), lambda j, i: (0, j))

    def gate(k):
        return pl.BlockSpec((tm, tn), lambda j, i, k=k: (i, k * nb + j))

    return pl.pallas_call(
        _merge_kernel,
        grid=(nb, m // tm),
        in_specs=[lhs(oa), lhs(ob), lhs(oc), rhs(wa), rhs(wb), rhs(wc), gate(0), gate(1), gate(2)],
        out_specs=pl.BlockSpec((tm, tn), lambda j, i: (i, j)),
        out_shape=jax.ShapeDtypeStruct((m, n), BF16),
        compiler_params=_cparams("parallel", "parallel"),
        name="merge",
    )(oa, ob, oc, wa, wb, wc, gates, gates, gates)


def _pack_bf16_pairs(x):
    n = x.shape[1] // 2
    lo = lax.bitcast_convert_type(x[:, :n].astype(BF16).astype(F32), jnp.uint32)
    hi = lax.bitcast_convert_type(x[:, n:].astype(BF16).astype(F32), jnp.uint32)
    return (hi & jnp.uint32(0xFFFF0000)) | (lo >> 16)


def _unpack_bf16_pairs(u):
    lo = lax.bitcast_convert_type(u << 16, F32)
    hi = lax.bitcast_convert_type(u & jnp.uint32(0xFFFF0000), F32)
    return lo, hi


def _router_kernel(x_ref, g_ref, w_ref, b_ref, xq_ref, route_ref):
    x = x_ref[...]
    xn = x * lax.rsqrt(jnp.mean(x * x, axis=-1, keepdims=True) + EPS) * g_ref[...]
    xq_ref[...] = _pack_bf16_pairs(xn)
    logits = jnp.dot(xn, w_ref[...], preferred_element_type=F32, precision=lax.Precision.HIGHEST) + b_ref[...]
    lane = lax.broadcasted_iota(jnp.int32, logits.shape, 1).astype(F32)
    none = float(LANES)

    def first_argmax(vals):
        top = jnp.max(vals, axis=1, keepdims=True)
        return top, jnp.min(jnp.where(vals == top, lane, none), axis=1, keepdims=True)

    is_group = lane < N_GROUPS
    g_top, g_idx = first_argmax(jnp.where(is_group, logits, NEG_BIG))
    g_w = 1.0 / jnp.sum(jnp.where(is_group, jnp.exp(logits - g_top), 0.0), axis=1, keepdims=True)
    lo = N_GROUPS + EXPERTS_PER_GROUP * g_idx
    e_logits = jnp.where((lane >= lo) & (lane < lo + EXPERTS_PER_GROUP), logits, NEG_BIG)
    v1, i1 = first_argmax(e_logits)
    v2, i2 = first_argmax(jnp.where(lane == i1, NEG_BIG, e_logits))
    e2 = jnp.exp(v2 - v1)
    w1 = g_w / (1.0 + e2)
    w2 = w1 * e2
    route = jnp.where(lane == 0, i1 - N_GROUPS, jnp.where(lane == 1, i2 - N_GROUPS,
                      jnp.where(lane == 2, w1, jnp.where(lane == 3, w2, 0.0))))
    route_ref[...] = route


def _router(x, g, w_router, b_router, tm=256):
    t, d = x.shape
    return pl.pallas_call(
        _router_kernel,
        grid=(t // tm,),
        in_specs=[pl.BlockSpec((tm, d), lambda i: (i, 0)), pl.BlockSpec((1, d), lambda i: (0, 0)),
                  pl.BlockSpec((d, LANES), lambda i: (0, 0)), pl.BlockSpec((1, LANES), lambda i: (0, 0))],
        out_specs=[pl.BlockSpec((tm, d // 2), lambda i: (i, 0)), pl.BlockSpec((tm, LANES), lambda i: (i, 0))],
        out_shape=[jax.ShapeDtypeStruct((t, d // 2), jnp.uint32), jax.ShapeDtypeStruct((t, LANES), F32)],
        compiler_params=_cparams("parallel"),
        name="router",
    )(x, g.reshape(1, d), w_router, b_router)


def _gather_rows(src_hbm, dst_ref, idx_ref, base, n_rows, sem):
    def body(r, carry):
        pltpu.make_async_copy(src_hbm.at[pl.ds(idx_ref[base + r], 1)], dst_ref.at[pl.ds(r, 1)], sem).start()
        return carry
    lax.fori_loop(0, n_rows, body, 0)


def _wait_rows(src_hbm, dst_ref, n_rows, sem):
    pltpu.make_async_copy(src_hbm.at[pl.ds(0, n_rows)], dst_ref, sem).wait()


def _expert_kernel(tile_expert_ref, row_token_ref, n_used_ref, x_hbm, w1_ref, w3_ref, w2_ref, rw_ref, y_ref,
                   xbuf, sem):
    del tile_expert_ref
    i = pl.program_id(0)
    n_used = n_used_ref[0]
    tm = MOE_TILE
    half = w1_ref.shape[1] // 2
    slot = i % 2

    @pl.when((i == 0) & (n_used > 0))
    def _():
        _gather_rows(x_hbm, xbuf.at[0], row_token_ref, 0, tm, sem.at[0])

    @pl.when(i + 1 < n_used)
    def _():
        _gather_rows(x_hbm, xbuf.at[1 - slot], row_token_ref, (i + 1) * tm, tm, sem.at[1 - slot])

    @pl.when(i < n_used)
    def _():
        _wait_rows(x_hbm, xbuf.at[slot], tm, sem.at[slot])
        lo, hi = _unpack_bf16_pairs(xbuf[slot])
        lo = lo.astype(BF16)
        hi = hi.astype(BF16)
        h1 = _dot(lo, w1_ref[0, 0:half, :]) + _dot(hi, w1_ref[0, half:2 * half, :])
        h3 = _dot(lo, w3_ref[0, 0:half, :]) + _dot(hi, w3_ref[0, half:2 * half, :])
        h = (h1 * _sigmoid(h1) * h3 * rw_ref[0]).astype(BF16)
        y_ref[...] = _pack_bf16_pairs(_dot(h, w2_ref[0]))

    @pl.when(i >= n_used)
    def _():
        y_ref[...] = jnp.zeros_like(y_ref)


def _experts(xq, w1, w3, w2, tile_expert, row_token, n_used, row_weight):
    t, dh = xq.shape
    d = 2 * dh
    ff = w1.shape[2]
    tm = MOE_TILE
    n_tiles = tile_expert.shape[0]
    grid_spec = pltpu.PrefetchScalarGridSpec(
        num_scalar_prefetch=3,
        grid=(n_tiles,),
        in_specs=[pl.BlockSpec(memory_space=pl.ANY),
                  pl.BlockSpec((1, d, ff), lambda i, te, rt, nu: (te[i], 0, 0)),
                  pl.BlockSpec((1, d, ff), lambda i, te, rt, nu: (te[i], 0, 0)),
                  pl.BlockSpec((1, ff, d), lambda i, te, rt, nu: (te[i], 0, 0)),
                  pl.BlockSpec((1, tm, 1), lambda i, te, rt, nu: (i, 0, 0))],
        out_specs=pl.BlockSpec((tm, dh), lambda i, te, rt, nu: (i, 0)),
        scratch_shapes=[pltpu.VMEM((2, tm, dh), jnp.uint32), pltpu.SemaphoreType.DMA((2,))],
    )
    return pl.pallas_call(
        _expert_kernel,
        grid_spec=grid_spec,
        out_shape=jax.ShapeDtypeStruct((n_tiles * tm, dh), jnp.uint32),
        compiler_params=_cparams("arbitrary"),
        name="experts",
    )(tile_expert, row_token, n_used, xq, w1, w3, w2, row_weight.reshape(n_tiles, tm, 1))


def _combine_kernel(pos_ref, y_hbm, x_ref, *rest):
    if len(rest) == 5:
        g_ref, xo_ref, xn_ref, ybuf, sem = rest
    else:
        g_ref, xn_ref = None, None
        xo_ref, ybuf, sem = rest
    i = pl.program_id(0)
    n = pl.num_programs(0)
    tc = COMBINE_TILE
    slot = i % 2

    @pl.when(i == 0)
    def _():
        _gather_rows(y_hbm, ybuf.at[0], pos_ref, 0, 2 * tc, sem.at[0])

    @pl.when(i + 1 < n)
    def _():
        _gather_rows(y_hbm, ybuf.at[1 - slot], pos_ref, (i + 1) * 2 * tc, 2 * tc, sem.at[1 - slot])

    _wait_rows(y_hbm, ybuf.at[slot], 2 * tc, sem.at[slot])
    lo0, hi0 = _unpack_bf16_pairs(ybuf[slot, 0:tc, :])
    lo1, hi1 = _unpack_bf16_pairs(ybuf[slot, tc:2 * tc, :])
    x = x_ref[...] + jnp.concatenate([lo0 + lo1, hi0 + hi1], axis=1)
    xo_ref[...] = x
    if xn_ref is not None:
        y = x * lax.rsqrt(jnp.mean(x * x, axis=-1, keepdims=True) + EPS) * g_ref[...]
        xn_ref[...] = y.astype(xn_ref.dtype)


def _combine(y_sorted, x, pos_tiles, next_g=None):
    t, d = x.shape
    tc = COMBINE_TILE
    tile = pl.BlockSpec((tc, d), lambda i, p: (i, 0))
    in_specs = [pl.BlockSpec(memory_space=pl.ANY), tile]
    out_specs = [tile]
    out_shape = [jax.ShapeDtypeStruct((t, d), F32)]
    args = [pos_tiles, y_sorted, x]
    if next_g is not None:
        in_specs.append(pl.BlockSpec((1, d), lambda i, p: (0, 0)))
        out_specs.append(tile)
        out_shape.append(jax.ShapeDtypeStruct((t, d), BF16))
        args.append(next_g.reshape(1, d))
    grid_spec = pltpu.PrefetchScalarGridSpec(
        num_scalar_prefetch=1,
        grid=(t // tc,),
        in_specs=in_specs,
        out_specs=out_specs,
        scratch_shapes=[pltpu.VMEM((2, 2 * tc, d // 2), jnp.uint32), pltpu.SemaphoreType.DMA((2,))],
    )
    out = pl.pallas_call(
        _combine_kernel,
        grid_spec=grid_spec,
        out_shape=out_shape,
        compiler_params=_cparams("arbitrary"),
        name="combine",
    )(*args)
    return (out[0], out[1]) if next_g is not None else (out[0], None)


def _route_plan(expert_ids, weights, n_tokens):
    tm, tc = MOE_TILE, COMBINE_TILE
    n_assign = 2 * n_tokens
    n_tiles = n_assign // tm + N_EXPERTS
    e_flat = expert_ids.reshape(-1)
    onehot = (e_flat[:, None] == jnp.arange(N_EXPERTS, dtype=jnp.int32)[None, :]).astype(jnp.int32)
    csum = jnp.cumsum(onehot, axis=0)
    rank = jnp.sum(csum * onehot, axis=1) - 1
    counts = csum[-1]
    padded = ((counts + tm - 1) // tm) * tm
    ends = jnp.cumsum(padded)
    offs = ends - padded
    pos = jnp.sum(onehot * offs[None, :], axis=1) + rank
    row_token = jnp.zeros((n_tiles * tm,), jnp.int32).at[pos].set(jnp.arange(n_assign, dtype=jnp.int32) // 2)
    row_weight = jnp.zeros((n_tiles * tm,), F32).at[pos].set(weights.reshape(-1))
    tile_start = jnp.arange(n_tiles, dtype=jnp.int32) * tm
    tile_expert = jnp.minimum(jnp.sum((ends[None, :] <= tile_start[:, None]).astype(jnp.int32), axis=1),
                              N_EXPERTS - 1)
    pos_tiles = pos.reshape(n_tokens // tc, tc, 2).transpose(0, 2, 1).reshape(-1)
    n_used = (ends[-1:] // tm).astype(jnp.int32)
    return tile_expert, row_token, n_used, row_weight, pos_tiles


def kernel(x, norm1_g, w_in, hgrn_lb, hgrn_norm_g, mlstm_conv_w, mlstm_i_bias, mlstm_f_bias, mlstm_norm_g,
           diff_qn_g, diff_kn_g, diff_lam_q1, diff_lam_k1, diff_lam_q2, diff_lam_k2, diff_subln_g, w_branch_a,
           w_branch_b, w_branch_c, w_out, norm2_g, router_group_w, router_group_b, router_expert_w,
           router_expert_b, expert_w1, expert_w3, expert_w2):
    bsz, s_len, d = x.shape
    depth = w_in.shape[0]
    t = bsz * s_len
    a_w, b_w, c_w = A_HEADS * HEAD_DIM, B_HEADS * HEAD_DIM, C_HEADS * HEAD_DIM
    o_b = 4 * a_w
    o_s = o_b + 4 * b_w
    o_c = o_s + 2 * B_HEADS
    o_g = o_c + 3 * c_w

    tables = _rope_tables(s_len)
    p_lb = jax.nn.softmax(hgrn_lb.astype(F32), axis=0)
    lb_all = jnp.cumsum(p_lb, axis=0) - p_lb[0:1]

    xf = x.reshape(t, d)
    xn = _rmsnorm(xf, norm1_g[0], BF16)
    for l in range(depth):
        wl = w_in[l]
        h_a = _matmul(xn, wl[:, :o_b].astype(BF16), BF16, name="in_proj_a").reshape(bsz, s_len, -1)
        h_b = _matmul(xn, wl[:, o_b:o_s].astype(BF16), BF16, name="in_proj_b").reshape(bsz, s_len, -1)
        h_c = _matmul(xn, wl[:, o_c:o_g].astype(BF16), BF16, name="in_proj_c").reshape(bsz, s_len, -1)
        gates = _matmul(xn, wl[:, o_g:].astype(BF16), BF16, name="in_proj_gates")
        w_small = jnp.pad(wl[:, o_s:o_c], ((0, 0), (0, LANES - 2 * B_HEADS))).astype(BF16)
        h_s = _matmul(xn, w_small, F32, name="in_proj_scalar_gates").reshape(bsz, s_len, LANES)
        gate_i = h_s[..., :B_HEADS] + mlstm_i_bias[l]
        gate_f = h_s[..., B_HEADS:2 * B_HEADS] + mlstm_f_bias[l]

        out_a = _hgrn2(h_a, lb_all[l], hgrn_norm_g[l])
        out_b = _mlstm(h_b, mlstm_conv_w[l], gate_i, gate_f, mlstm_norm_g[l])
        lambda_init = 0.8 - 0.6 * math.exp(-0.3 * l)
        lam_vecs = jnp.stack([diff_lam_q1[l], diff_lam_k1[l], diff_lam_q2[l], diff_lam_k2[l]]).astype(F32)
        out_c = _diff_attn(h_c, tables, diff_qn_g[l], diff_kn_g[l], lam_vecs, diff_subln_g[l], lambda_init)

        y = _merge(out_a.reshape(t, a_w), out_b.reshape(t, b_w), out_c.reshape(t, c_w),
                   w_branch_a[l].astype(BF16), w_branch_b[l].astype(BF16), w_branch_c[l].astype(BF16), gates)
        xf = _matmul(y, w_out[l].astype(BF16), F32, residual=xf, name="out_proj")

        n_router = N_GROUPS + N_EXPERTS
        w_router = jnp.pad(jnp.concatenate([router_group_w[l], router_expert_w[l]], axis=1),
                           ((0, 0), (0, LANES - n_router)))
        b_router = jnp.pad(jnp.concatenate([router_group_b[l], router_expert_b[l]]),
                           (0, LANES - n_router)).reshape(1, LANES)
        xq, route = _router(xf, norm2_g[l], w_router, b_router)
        tile_expert, row_token, n_used, row_weight, pos_tiles = _route_plan(
            route[:, 0:2].astype(jnp.int32), route[:, 2:4], t)
        ff = expert_w1.shape[-1]
        y_sorted = _experts(xq, expert_w1[l].reshape(N_EXPERTS, d, ff).astype(BF16),
                            expert_w3[l].reshape(N_EXPERTS, d, ff).astype(BF16),
                            expert_w2[l].reshape(N_EXPERTS, ff, d).astype(BF16),
                            tile_expert, row_token, n_used, row_weight)
        xf, xn = _combine(y_sorted, xf, pos_tiles, norm1_g[l + 1] if l + 1 < depth else None)
    return xf.reshape(bsz, s_len, d)
```

```python
import functools
import math

import jax
import jax.numpy as jnp
from jax import lax
from jax.experimental import pallas as pl
from jax.experimental.pallas import tpu as pltpu

F32 = jnp.float32
BF16 = jnp.bfloat16

HEAD_DIM = 128
A_HEADS = 8
B_HEADS = 12
C_HEADS = 12
C_DQK = 64
CONV_K = 4
ROPE_THETA = 500000.0
ROPE_DIM = C_DQK // 4
ROPE_HALF = ROPE_DIM // 2
N_GROUPS = 4
EXPERTS_PER_GROUP = 8
N_EXPERTS = N_GROUPS * EXPERTS_PER_GROUP
EPS = 1e-6
LANES = 128
NEG_BIG = -1e30

VMEM_LIMIT = 56 * 1024 * 1024

HGRN_CHUNK = 128
HGRN_SUB = 32
HGRN_EXP_CLAMP = 60.0
MLSTM_CHUNK = 128
ATT_BLOCK = 256
MASK_CHUNK = 64
MOE_TILE = 256
COMBINE_TILE = 128
GATHER_UNROLL = 8


def _cparams(*sem):
    return pltpu.CompilerParams(dimension_semantics=sem, vmem_limit_bytes=VMEM_LIMIT)


def _sigmoid(x):
    return 1.0 / (1.0 + jnp.exp(-x))


def _dot(a, b):
    return jnp.dot(a, b, preferred_element_type=F32)


def _dot_nt(a, b):
    return lax.dot_general(a, b, (((1,), (1,)), ((), ())), preferred_element_type=F32)


def _dot_tn(a, b):
    return lax.dot_general(a, b, (((0,), (0,)), ((), ())), preferred_element_type=F32)


def _split_dot(exact_lhs, x):
    hi = x.astype(BF16)
    lo = (x - hi.astype(F32)).astype(BF16)
    return _dot(exact_lhs, hi) + _dot(exact_lhs, lo)


def _split_dot_rhs(x, exact_rhs):
    hi = x.astype(BF16)
    lo = (x - hi.astype(F32)).astype(BF16)
    return _dot(hi, exact_rhs) + _dot(lo, exact_rhs)


def _rmsnorm_kernel(x_ref, g_ref, o_ref):
    x = x_ref[...]
    y = x * lax.rsqrt(jnp.mean(x * x, axis=-1, keepdims=True) + EPS) * g_ref[...]
    o_ref[...] = y.astype(o_ref.dtype)


def _rmsnorm(x, g, out_dtype, tm=256):
    t, d = x.shape
    return pl.pallas_call(
        _rmsnorm_kernel,
        grid=(t // tm,),
        in_specs=[pl.BlockSpec((tm, d), lambda i: (i, 0)), pl.BlockSpec((1, d), lambda i: (0, 0))],
        out_specs=pl.BlockSpec((tm, d), lambda i: (i, 0)),
        out_shape=jax.ShapeDtypeStruct((t, d), out_dtype),
        compiler_params=_cparams("parallel"),
        name="rmsnorm",
    )(x, g.reshape(1, d))


def _mm_kernel(x_ref, w_ref, *rest, has_residual, cast_weight):
    r_ref = rest[0] if has_residual else None
    o_ref = rest[1] if has_residual else rest[0]
    if cast_weight:
        wb_ref = rest[-1]

        @pl.when(pl.program_id(1) == 0)
        def _():
            wb_ref[...] = w_ref[0].astype(BF16)

        acc = _dot(x_ref[...], wb_ref[...])
    else:
        acc = _dot(x_ref[...], w_ref[0])
    if has_residual:
        acc = acc + r_ref[...]
    o_ref[...] = acc.astype(o_ref.dtype)


def _pick_tile(n, candidates):
    for c in candidates:
        if n % c == 0:
            return c
    return n


def _matmul(x, w, layer, out_dtype, col0=0, ncols=None, residual=None, name="matmul"):
    m, k = x.shape
    n = w.shape[2] - col0 if ncols is None else ncols
    cast_weight = w.dtype != BF16
    tm = _pick_tile(m, (1024, 512, 256, 128))
    tn = _pick_tile(math.gcd(n, col0) if col0 else n, (512,) if cast_weight else (1024, 768, 512, 256, 128))
    cb = col0 // tn
    in_specs = [pl.BlockSpec((tm, k), lambda j, i: (i, 0)),
                pl.BlockSpec((1, k, tn), lambda j, i: (layer, 0, cb + j))]
    args = [x, w]
    if residual is not None:
        in_specs.append(pl.BlockSpec((tm, tn), lambda j, i: (i, j)))
        args.append(residual)
    return pl.pallas_call(
        functools.partial(_mm_kernel, has_residual=residual is not None, cast_weight=cast_weight),
        grid=(n // tn, m // tm),
        in_specs=in_specs,
        out_specs=pl.BlockSpec((tm, tn), lambda j, i: (i, j)),
        out_shape=jax.ShapeDtypeStruct((m, n), out_dtype),
        scratch_shapes=[pltpu.VMEM((k, tn), BF16)] if cast_weight else [],
        compiler_params=_cparams("parallel", "arbitrary"),
        name=name,
    )(*args)


def _hgrn2_head(qp, z, v, gp, lb, ng, st):
    ch, sb = HGRN_CHUNK, HGRN_SUB
    nsb = ch // sb
    sig = _sigmoid(z)
    logf = jnp.log(lb + (1.0 - lb) * sig)
    k = (1.0 - lb) * (1.0 - sig)
    q = qp * _sigmoid(qp)

    row = lax.broadcasted_iota(jnp.int32, (ch, ch), 0)
    col = lax.broadcasted_iota(jnp.int32, (ch, ch), 1)
    causal = row >= col
    tri = jnp.where(causal, 1.0, 0.0).astype(BF16)
    b = _split_dot(tri, logf)
    b_last = b[ch - 1:ch, :]

    o = _dot_nt((q * jnp.exp(b)).astype(BF16), st.astype(BF16))

    refs = [b[i * sb:i * sb + 1, :] for i in range(nsb)]
    rblk = jnp.concatenate([jnp.broadcast_to(r, (sb, HEAD_DIM)) for r in refs], axis=0)
    qt = q * jnp.exp(b - rblk)
    rowblk = lax.broadcasted_iota(jnp.int32, (ch, HEAD_DIM), 0) // sb
    qcat = jnp.concatenate([jnp.where(rowblk == i, qt, 0.0).astype(BF16) for i in range(nsb)], axis=1)
    kcat = jnp.concatenate(
        [(k * jnp.exp(jnp.minimum(r - b, HGRN_EXP_CLAMP))).astype(BF16) for r in refs], axis=1)
    s = jnp.where(causal, _dot_nt(qcat, kcat), 0.0)
    o = o + _dot(s.astype(BF16), v)

    kd = (k * jnp.exp(b_last - b)).astype(BF16)
    st_new = st * jnp.exp(b_last) + _dot_tn(v, kd)

    y = o * lax.rsqrt(jnp.mean(o * o, axis=-1, keepdims=True) + EPS) * ng
    return y * (gp * _sigmoid(gp)), st_new


def _hgrn2_kernel(q_ref, f_ref, i_ref, g_ref, lb_ref, ng_ref, o_ref, st_ref):
    @pl.when(pl.program_id(1) == 0)
    def _():
        st_ref[...] = jnp.zeros_like(st_ref)

    heads = range(A_HEADS)
    state = [st_ref[h] for h in heads]
    new_state = []
    for h in heads:
        cols = slice(h * HEAD_DIM, (h + 1) * HEAD_DIM)
        y, st = _hgrn2_head(q_ref[0, :, cols].astype(F32), f_ref[0, :, cols].astype(F32), i_ref[0, :, cols],
                            g_ref[0, :, cols].astype(F32), lb_ref[:, cols], ng_ref[...], state[h])
        new_state.append(st)
        o_ref[0, :, cols] = y.astype(o_ref.dtype)
    for h in heads:
        st_ref[h] = new_state[h]


def _hgrn2(h_a, lb, norm_g):
    bsz, s_len, _ = h_a.shape
    ch = HGRN_CHUNK
    a_w = A_HEADS * HEAD_DIM

    def col(j):
        return pl.BlockSpec((1, ch, a_w), lambda b, c, j=j: (b, c, j))

    return pl.pallas_call(
        _hgrn2_kernel,
        grid=(bsz, s_len // ch),
        in_specs=[col(0), col(1), col(2), col(3),
                  pl.BlockSpec((1, a_w), lambda b, c: (0, 0)),
                  pl.BlockSpec((1, HEAD_DIM), lambda b, c: (0, 0))],
        out_specs=pl.BlockSpec((1, ch, a_w), lambda b, c: (b, c, 0)),
        out_shape=jax.ShapeDtypeStruct((bsz, s_len, a_w), BF16),
        scratch_shapes=[pltpu.VMEM((A_HEADS, HEAD_DIM, HEAD_DIM), F32)],
        compiler_params=_cparams("parallel", "arbitrary"),
        name="hgrn2",
    )(h_a, h_a, h_a, h_a, lb.reshape(1, -1), norm_g.reshape(1, HEAD_DIM))


def _log_sigmoid(x):
    return jnp.minimum(x, 0.0) - jnp.log(1.0 + jnp.exp(-jnp.abs(x)))


def _causal_conv_silu(x, prev, w):
    rowi = lax.broadcasted_iota(jnp.int32, x.shape, 0)
    acc = x * w[CONV_K - 1:CONV_K, :]
    for sft in range(1, CONV_K):
        shifted = jnp.where(rowi >= sft, pltpu.roll(x, sft, 0), pltpu.roll(prev, sft, 0))
        acc = acc + shifted * w[CONV_K - 1 - sft:CONV_K - sft, :]
    return acc * _sigmoid(acc)


def _mlstm_kernel(q_ref, k_ref, v_ref, og_ref, w_ref, gir_ref, gfr_ref, gic_ref, gfc_ref, ng_ref,
                  o_ref, c_ref, n_ref, m_ref, qprev_ref, kprev_ref):
    @pl.when(pl.program_id(1) == 0)
    def _():
        c_ref[...] = jnp.zeros_like(c_ref)
        n_ref[...] = jnp.zeros_like(n_ref)
        m_ref[...] = jnp.zeros_like(m_ref)
        qprev_ref[...] = jnp.zeros_like(qprev_ref)
        kprev_ref[...] = jnp.zeros_like(kprev_ref)

    ch = MLSTM_CHUNK
    b_w = B_HEADS * HEAD_DIM
    heads = range(B_HEADS)
    row = lax.broadcasted_iota(jnp.int32, (ch, ch), 0)
    col = lax.broadcasted_iota(jnp.int32, (ch, ch), 1)
    tril = row >= col
    triu = row <= col

    def cols(h):
        return slice(h * HEAD_DIM, (h + 1) * HEAD_DIM)

    qprev = qprev_ref[...]
    kprev = kprev_ref[...]
    qc, kc, g, inter, logw_c, logw_r, m_prev, f_last = [], [], [], [], [], [], [], []
    for h in heads:
        qc.append(_causal_conv_silu(q_ref[0, :, cols(h)].astype(F32), qprev[:, cols(h)], w_ref[:, cols(h)])
                  * (HEAD_DIM ** -0.5))
        kc.append(_causal_conv_silu(k_ref[0, :, cols(h)].astype(F32), kprev[:, cols(h)],
                                    w_ref[:, b_w + h * HEAD_DIM:b_w + (h + 1) * HEAD_DIM]))
        li_r = gir_ref[0, 0, h:h + 1, :]
        li_c = gic_ref[0, :, h:h + 1]
        lf_r = _log_sigmoid(gfr_ref[0, 0, h:h + 1, :])
        lf_c = _log_sigmoid(gfc_ref[0, :, h:h + 1])
        fc_c = jnp.sum(jnp.where(tril, lf_r, 0.0), axis=1, keepdims=True)
        fc_r = jnp.sum(jnp.where(triu, lf_c, 0.0), axis=0, keepdims=True)
        g.append(jnp.where(tril, fc_c - fc_r + li_r, NEG_BIG))
        m_prev.append(m_ref[h])
        inter.append(fc_c + m_prev[h])
        f_last.append(fc_c[ch - 1:ch, :])
        logw_c.append(f_last[h] - fc_c + li_c)
        logw_r.append(f_last[h] - fc_r + li_r)
    qprev_ref[...] = q_ref[0].astype(F32)
    kprev_ref[...] = k_ref[0].astype(F32)

    m, num, den, m_new = [], [], [], []
    for h in heads:
        m.append(jnp.maximum(jnp.max(g[h], axis=1, keepdims=True), inter[h]))
        m_new.append(jnp.maximum(f_last[h] + m_prev[h], jnp.max(logw_r[h], axis=1, keepdims=True)))
    for h in heads:
        dmat = jnp.exp(g[h] - m[h])
        a = jnp.exp(inter[h] - m[h])
        qb = qc[h].astype(BF16)
        sc = _dot_nt(qb, kc[h].astype(BF16)) * dmat
        num.append(_dot(sc.astype(BF16), v_ref[0, :, cols(h)]) + a * _dot(qb, c_ref[h].astype(BF16)))
        den.append(jnp.sum(sc, axis=1, keepdims=True) + a * jnp.sum(qc[h] * n_ref[h], axis=1, keepdims=True))

    for h in heads:
        dec = jnp.exp(f_last[h] + m_prev[h] - m_new[h])
        kw = kc[h] * jnp.exp(logw_c[h] - m_new[h])
        c_ref[h] = dec * c_ref[h] + _dot_tn(kw.astype(BF16), v_ref[0, :, cols(h)])
        n_ref[h] = dec * n_ref[h] + jnp.sum(kw, axis=0, keepdims=True)
        m_ref[h] = m_new[h]

    hs = [num[h] / jnp.maximum(jnp.abs(den[h]), jnp.exp(-m[h])) for h in heads]
    ms = [jnp.mean(hs[h] * hs[h], axis=-1, keepdims=True) for h in heads]
    for h in heads:
        y = hs[h] * lax.rsqrt(ms[h] + EPS) * ng_ref[...]
        o_ref[0, :, cols(h)] = (y * _sigmoid(og_ref[0, :, cols(h)].astype(F32))).astype(o_ref.dtype)


def _mlstm(h_b, conv_w, gate_i, gate_f, norm_g):
    bsz, s_len, _ = h_b.shape
    ch = MLSTM_CHUNK
    nc = s_len // ch
    b_w = B_HEADS * HEAD_DIM

    def col(j):
        return pl.BlockSpec((1, ch, b_w), lambda b, c, j=j: (b, c, j))

    def rows(gt):
        return gt.reshape(bsz, nc, ch, B_HEADS).transpose(0, 1, 3, 2)

    row_spec = pl.BlockSpec((1, 1, B_HEADS, ch), lambda b, c: (b, c, 0, 0))
    col_spec = pl.BlockSpec((1, ch, B_HEADS), lambda b, c: (b, c, 0))
    return pl.pallas_call(
        _mlstm_kernel,
        grid=(bsz, nc),
        in_specs=[col(0), col(1), col(2), col(3),
                  pl.BlockSpec((CONV_K, 2 * b_w), lambda b, c: (0, 0)),
                  row_spec, row_spec, col_spec, col_spec,
                  pl.BlockSpec((1, HEAD_DIM), lambda b, c: (0, 0))],
        out_specs=pl.BlockSpec((1, ch, b_w), lambda b, c: (b, c, 0)),
        out_shape=jax.ShapeDtypeStruct((bsz, s_len, b_w), BF16),
        scratch_shapes=[pltpu.VMEM((B_HEADS, HEAD_DIM, HEAD_DIM), F32), pltpu.VMEM((B_HEADS, 1, HEAD_DIM), F32),
                        pltpu.VMEM((B_HEADS, 1, 1), F32), pltpu.VMEM((ch, b_w), F32),
                        pltpu.VMEM((ch, b_w), F32)],
        compiler_params=_cparams("parallel", "arbitrary"),
        name="mlstm",
    )(h_b, h_b, h_b, h_b, conv_w, rows(gate_i), rows(gate_f), gate_i, gate_f, norm_g.reshape(1, HEAD_DIM))


def _diff_attn_kernel(q_ref, k_ref, v_ref, cos_ref, sa_ref, sb_ref, qg_ref, kg_ref, lam_ref, sg_ref,
                      o_ref, qs_ref, kp_ref, vx_ref, *, lambda_init):
    s_len = q_ref.shape[1]
    tq = ATT_BLOCK
    nq = s_len // tq
    lane = lax.broadcasted_iota(jnp.int32, (tq, LANES), 1)
    first_map = lane < C_DQK
    li = lax.broadcasted_iota(jnp.int32, (LANES, LANES), 0) // C_DQK
    lj = lax.broadcasted_iota(jnp.int32, (LANES, LANES), 1) // C_DQK
    same_map = jnp.where(li == lj, 1.0, 0.0).astype(BF16)

    def qk_norm_rope(x, g, rows):
        ss = _split_dot_rhs(x * x, same_map)
        y = x * lax.rsqrt(ss * (1.0 / C_DQK) + EPS) * g
        return (y * cos_ref[rows, :] + pltpu.roll(y, LANES - ROPE_HALF, 1) * sa_ref[rows, :]
                + pltpu.roll(y, ROPE_HALF, 1) * sb_ref[rows, :])

    def prep(i, carry):
        rows = pl.ds(pl.multiple_of(i * tq, tq), tq)
        qn = qk_norm_rope(q_ref[0, rows, :].astype(F32), qg_ref[...], rows) * (C_DQK ** -0.5)
        qs_ref[i, 0:tq, :] = jnp.where(first_map, qn, 0.0).astype(BF16)
        qs_ref[i, tq:2 * tq, :] = jnp.where(first_map, 0.0, qn).astype(BF16)
        kp_ref[rows, :] = qk_norm_rope(k_ref[0, rows, :].astype(F32), kg_ref[...], rows).astype(BF16)
        vx_ref[rows, 0:HEAD_DIM] = v_ref[0, rows, :]
        vx_ref[rows, HEAD_DIM:2 * HEAD_DIM] = jnp.ones((tq, HEAD_DIM), BF16)
        return carry

    lax.fori_loop(0, nq, prep, 0)

    lamv = lam_ref[...]
    lam = (jnp.exp(jnp.sum(lamv[0:1] * lamv[1:2], axis=1, keepdims=True))
           - jnp.exp(jnp.sum(lamv[2:3] * lamv[3:4], axis=1, keepdims=True)) + lambda_init)
    qchunk = (lax.broadcasted_iota(jnp.int32, (2 * tq, tq), 0) % tq) // MASK_CHUNK
    kchunk = lax.broadcasted_iota(jnp.int32, (2 * tq, tq), 1) // MASK_CHUNK
    diag_ok = qchunk >= kchunk

    for i in range(nq):
        k0 = i * tq
        qs = qs_ref[i]
        s_d = jnp.where(diag_ok, _dot_nt(qs, kp_ref[k0:k0 + tq, :]), NEG_BIG)
        m = jnp.max(s_d, axis=1, keepdims=True)
        if i > 0:
            s_o = _dot_nt(qs, kp_ref[0:k0, :])
            m = jnp.maximum(m, jnp.max(s_o, axis=1, keepdims=True))
            acc = _dot(jnp.exp(s_o - m).astype(BF16), vx_ref[0:k0, :])
            acc = acc + _dot(jnp.exp(s_d - m).astype(BF16), vx_ref[k0:k0 + tq, :])
        else:
            acc = _dot(jnp.exp(s_d - m).astype(BF16), vx_ref[k0:k0 + tq, :])
        o = acc[:, 0:HEAD_DIM] / acc[:, HEAD_DIM:2 * HEAD_DIM]
        out = o[0:tq] - lam * o[tq:2 * tq]
        y = out * lax.rsqrt(jnp.mean(out * out, axis=-1, keepdims=True) + EPS) * sg_ref[...]
        o_ref[0, k0:k0 + tq, :] = (y * (1.0 - lambda_init)).astype(o_ref.dtype)


def _rope_tables(s_len):
    pos = jnp.arange(s_len, dtype=F32)
    inv_freq = ROPE_THETA ** (-jnp.arange(0, ROPE_DIM, 2, dtype=F32) / ROPE_DIM)
    ang = pos[:, None] * inv_freq[None, :]
    cos, sin = jnp.cos(ang), jnp.sin(ang)
    ones = jnp.ones((s_len, C_DQK - ROPE_DIM), F32)
    zeros = jnp.zeros((s_len, C_DQK - ROPE_DIM), F32)
    zh = jnp.zeros((s_len, ROPE_HALF), F32)
    cos_map = jnp.concatenate([cos, cos, ones], axis=1)
    sa_map = jnp.concatenate([-sin, zh, zeros], axis=1)
    sb_map = jnp.concatenate([zh, sin, zeros], axis=1)
    return tuple(jnp.concatenate([t, t], axis=1) for t in (cos_map, sa_map, sb_map))


def _diff_attn(h_c, tables, qn_g, kn_g, lam_vecs, subln_g, lambda_init):
    bsz, s_len, _ = h_c.shape

    def col(j):
        return pl.BlockSpec((1, s_len, HEAD_DIM), lambda b, h, j=j: (b, 0, j * C_HEADS + h))

    def full(shape):
        return pl.BlockSpec(shape, lambda b, h: (0,) * len(shape))

    return pl.pallas_call(
        functools.partial(_diff_attn_kernel, lambda_init=lambda_init),
        grid=(bsz, C_HEADS),
        in_specs=[col(0), col(1), col(2), full((s_len, LANES)), full((s_len, LANES)), full((s_len, LANES)),
                  full((1, LANES)), full((1, LANES)), full((4, C_DQK)), full((1, HEAD_DIM))],
        out_specs=pl.BlockSpec((1, s_len, HEAD_DIM), lambda b, h: (b, 0, h)),
        out_shape=jax.ShapeDtypeStruct((bsz, s_len, C_HEADS * HEAD_DIM), BF16),
        scratch_shapes=[pltpu.VMEM((s_len // ATT_BLOCK, 2 * ATT_BLOCK, LANES), BF16),
                        pltpu.VMEM((s_len, LANES), BF16), pltpu.VMEM((s_len, 2 * HEAD_DIM), BF16)],
        compiler_params=_cparams("parallel", "parallel"),
        name="diff_attn",
    )(h_c, h_c, h_c, *tables, jnp.tile(qn_g, 2).reshape(1, LANES), jnp.tile(kn_g, 2).reshape(1, LANES),
      lam_vecs, subln_g.reshape(1, HEAD_DIM))


def _merge_kernel(oa_ref, ob_ref, oc_ref, wa_ref, wb_ref, wc_ref, ga_ref, gb_ref, gc_ref, y_ref,
                  wab_ref, wbb_ref, wcb_ref):
    @pl.when(pl.program_id(1) == 0)
    def _():
        wab_ref[...] = wa_ref[0].astype(BF16)
        wbb_ref[...] = wb_ref[0].astype(BF16)
        wcb_ref[...] = wc_ref[0].astype(BF16)

    y = _sigmoid(ga_ref[...].astype(F32)) * _dot(oa_ref[...], wab_ref[...])
    y = y + _sigmoid(gb_ref[...].astype(F32)) * _dot(ob_ref[...], wbb_ref[...])
    y = y + _sigmoid(gc_ref[...].astype(F32)) * _dot(oc_ref[...], wcb_ref[...])
    y_ref[...] = y.astype(y_ref.dtype)


def _merge(oa, ob, oc, wa, wb, wc, layer, gates):
    m = oa.shape[0]
    n = wa.shape[2]
    tm = _pick_tile(m, (1024, 512, 256, 128))
    tn = _pick_tile(n, (512, 256, 128))
    nb = n // tn

    def lhs(a):
        return pl.BlockSpec((tm, a.shape[1]), lambda j, i: (i, 0))

    def rhs(w):
        return pl.BlockSpec((1, w.shape[1], tn), lambda j, i: (layer, 0, j))

    def gate(k):
        return pl.BlockSpec((tm, tn), lambda j, i, k=k: (i, k * nb + j))

    return pl.pallas_call(
        _merge_kernel,
        grid=(nb, m // tm),
        in_specs=[lhs(oa), lhs(ob), lhs(oc), rhs(wa), rhs(wb), rhs(wc), gate(0), gate(1), gate(2)],
        out_specs=pl.BlockSpec((tm, tn), lambda j, i: (i, j)),
        out_shape=jax.ShapeDtypeStruct((m, n), BF16),
        scratch_shapes=[pltpu.VMEM((w.shape[1], tn), BF16) for w in (wa, wb, wc)],
        compiler_params=_cparams("parallel", "arbitrary"),
        name="merge",
    )(oa, ob, oc, wa, wb, wc, gates, gates, gates)


def _pack_bf16_pairs(x):
    n = x.shape[1] // 2
    lo = lax.bitcast_convert_type(x[:, :n].astype(BF16).astype(F32), jnp.uint32)
    hi = lax.bitcast_convert_type(x[:, n:].astype(BF16).astype(F32), jnp.uint32)
    return (hi & jnp.uint32(0xFFFF0000)) | (lo >> 16)


def _unpack_bf16_pairs(u):
    lo = lax.bitcast_convert_type(u << 16, F32)
    hi = lax.bitcast_convert_type(u & jnp.uint32(0xFFFF0000), F32)
    return lo, hi


def _router_kernel(x_ref, g_ref, w_ref, b_ref, xq_ref, route_ref):
    x = x_ref[...]
    xn = x * lax.rsqrt(jnp.mean(x * x, axis=-1, keepdims=True) + EPS) * g_ref[...]
    xq_ref[...] = _pack_bf16_pairs(xn)
    logits = jnp.dot(xn, w_ref[...], preferred_element_type=F32, precision=lax.Precision.HIGHEST) + b_ref[...]
    lane = lax.broadcasted_iota(jnp.int32, logits.shape, 1).astype(F32)
    none = float(LANES)

    def first_argmax(vals):
        top = jnp.max(vals, axis=1, keepdims=True)
        return top, jnp.min(jnp.where(vals == top, lane, none), axis=1, keepdims=True)

    is_group = lane < N_GROUPS
    g_top, g_idx = first_argmax(jnp.where(is_group, logits, NEG_BIG))
    g_w = 1.0 / jnp.sum(jnp.where(is_group, jnp.exp(logits - g_top), 0.0), axis=1, keepdims=True)
    lo = N_GROUPS + EXPERTS_PER_GROUP * g_idx
    e_logits = jnp.where((lane >= lo) & (lane < lo + EXPERTS_PER_GROUP), logits, NEG_BIG)
    v1, i1 = first_argmax(e_logits)
    v2, i2 = first_argmax(jnp.where(lane == i1, NEG_BIG, e_logits))
    e2 = jnp.exp(v2 - v1)
    w1 = g_w / (1.0 + e2)
    w2 = w1 * e2
    route = jnp.where(lane == 0, i1 - N_GROUPS, jnp.where(lane == 1, i2 - N_GROUPS,
                      jnp.where(lane == 2, w1, jnp.where(lane == 3, w2, 0.0))))
    route_ref[...] = route


def _router(x, g, w_router, b_router, tm=256):
    t, d = x.shape
    return pl.pallas_call(
        _router_kernel,
        grid=(t // tm,),
        in_specs=[pl.BlockSpec((tm, d), lambda i: (i, 0)), pl.BlockSpec((1, d), lambda i: (0, 0)),
                  pl.BlockSpec((d, LANES), lambda i: (0, 0)), pl.BlockSpec((1, LANES), lambda i: (0, 0))],
        out_specs=[pl.BlockSpec((tm, d // 2), lambda i: (i, 0)), pl.BlockSpec((tm, LANES), lambda i: (i, 0))],
        out_shape=[jax.ShapeDtypeStruct((t, d // 2), jnp.uint32), jax.ShapeDtypeStruct((t, LANES), F32)],
        compiler_params=_cparams("parallel"),
        name="router",
    )(x, g.reshape(1, d), w_router, b_router)


def _gather_rows(src_hbm, dst_ref, idx_ref, base, n_rows, sem):
    def body(j, carry):
        for u in range(GATHER_UNROLL):
            r = j * GATHER_UNROLL + u
            pltpu.make_async_copy(src_hbm.at[pl.ds(idx_ref[base + r], 1)], dst_ref.at[pl.ds(r, 1)],
                                  sem).start(priority=u % 2)
        return carry
    lax.fori_loop(0, n_rows // GATHER_UNROLL, body, 0)


def _wait_rows(src_hbm, dst_ref, n_rows, sem):
    pltpu.make_async_copy(src_hbm.at[pl.ds(0, n_rows)], dst_ref, sem).wait()


def _expert_kernel(tile_expert_ref, row_token_ref, n_used_ref, x_hbm, w1_ref, w3_ref, w2_ref, y_ref, xbuf, sem):
    del tile_expert_ref
    i = pl.program_id(0)
    n_used = n_used_ref[0]
    tm = MOE_TILE
    half = w1_ref.shape[1] // 2
    slot = i % 2

    @pl.when((i == 0) & (n_used > 0))
    def _():
        _gather_rows(x_hbm, xbuf.at[0], row_token_ref, 0, tm, sem.at[0])

    @pl.when(i + 1 < n_used)
    def _():
        _gather_rows(x_hbm, xbuf.at[1 - slot], row_token_ref, (i + 1) * tm, tm, sem.at[1 - slot])

    @pl.when(i < n_used)
    def _():
        _wait_rows(x_hbm, xbuf.at[slot], tm, sem.at[slot])
        lo, hi = _unpack_bf16_pairs(xbuf[slot])
        lo = lo.astype(BF16)
        hi = hi.astype(BF16)
        h1 = _dot(lo, w1_ref[0, 0:half, :]) + _dot(hi, w1_ref[0, half:2 * half, :])
        h3 = _dot(lo, w3_ref[0, 0:half, :]) + _dot(hi, w3_ref[0, half:2 * half, :])
        h = (h1 * _sigmoid(h1) * h3).astype(BF16)
        y_ref[...] = _pack_bf16_pairs(_dot(h, w2_ref[0]))

    @pl.when(i >= n_used)
    def _():
        y_ref[...] = jnp.zeros_like(y_ref)


def _experts(xq, w1, w3, w2, layer, tile_expert, row_token, n_used):
    t, dh = xq.shape
    d = 2 * dh
    ff = w1.shape[2]
    tm = MOE_TILE
    n_tiles = tile_expert.shape[0]
    e0 = layer * N_EXPERTS
    grid_spec = pltpu.PrefetchScalarGridSpec(
        num_scalar_prefetch=3,
        grid=(n_tiles,),
        in_specs=[pl.BlockSpec(memory_space=pl.ANY),
                  pl.BlockSpec((1, d, ff), lambda i, te, rt, nu: (e0 + te[i], 0, 0)),
                  pl.BlockSpec((1, d, ff), lambda i, te, rt, nu: (e0 + te[i], 0, 0)),
                  pl.BlockSpec((1, ff, d), lambda i, te, rt, nu: (e0 + te[i], 0, 0))],
        out_specs=pl.BlockSpec((tm, dh), lambda i, te, rt, nu: (i, 0)),
        scratch_shapes=[pltpu.VMEM((2, tm, dh), jnp.uint32), pltpu.SemaphoreType.DMA((2,))],
    )
    return pl.pallas_call(
        _expert_kernel,
        grid_spec=grid_spec,
        out_shape=jax.ShapeDtypeStruct((n_tiles * tm, dh), jnp.uint32),
        compiler_params=_cparams("arbitrary"),
        name="experts",
    )(tile_expert, row_token, n_used, xq, w1, w3, w2)


def _combine_kernel(pos_ref, y_hbm, x_ref, route_ref, *rest):
    if len(rest) == 5:
        g_ref, xo_ref, xn_ref, ybuf, sem = rest
    else:
        g_ref, xn_ref = None, None
        xo_ref, ybuf, sem = rest
    i = pl.program_id(0)
    n = pl.num_programs(0)
    tc = COMBINE_TILE
    slot = i % 2

    @pl.when(i == 0)
    def _():
        _gather_rows(y_hbm, ybuf.at[0], pos_ref, 0, 2 * tc, sem.at[0])

    @pl.when(i + 1 < n)
    def _():
        _gather_rows(y_hbm, ybuf.at[1 - slot], pos_ref, (i + 1) * 2 * tc, 2 * tc, sem.at[1 - slot])

    _wait_rows(y_hbm, ybuf.at[slot], 2 * tc, sem.at[slot])
    lo0, hi0 = _unpack_bf16_pairs(ybuf[slot, 0:tc, :])
    lo1, hi1 = _unpack_bf16_pairs(ybuf[slot, tc:2 * tc, :])
    w0 = route_ref[:, 2:3]
    w1 = route_ref[:, 3:4]
    x = x_ref[...] + jnp.concatenate([w0 * lo0 + w1 * lo1, w0 * hi0 + w1 * hi1], axis=1)
    xo_ref[...] = x
    if xn_ref is not None:
        y = x * lax.rsqrt(jnp.mean(x * x, axis=-1, keepdims=True) + EPS) * g_ref[...]
        xn_ref[...] = y.astype(xn_ref.dtype)


def _combine(y_sorted, x, route, pos_tiles, next_g=None):
    t, d = x.shape
    tc = COMBINE_TILE
    tile = pl.BlockSpec((tc, d), lambda i, p: (i, 0))
    in_specs = [pl.BlockSpec(memory_space=pl.ANY), tile, pl.BlockSpec((tc, LANES), lambda i, p: (i, 0))]
    out_specs = [tile]
    out_shape = [jax.ShapeDtypeStruct((t, d), F32)]
    args = [pos_tiles, y_sorted, x, route]
    if next_g is not None:
        in_specs.append(pl.BlockSpec((1, d), lambda i, p: (0, 0)))
        out_specs.append(tile)
        out_shape.append(jax.ShapeDtypeStruct((t, d), BF16))
        args.append(next_g.reshape(1, d))
    grid_spec = pltpu.PrefetchScalarGridSpec(
        num_scalar_prefetch=1,
        grid=(t // tc,),
        in_specs=in_specs,
        out_specs=out_specs,
        scratch_shapes=[pltpu.VMEM((2, 2 * tc, d // 2), jnp.uint32), pltpu.SemaphoreType.DMA((2,))],
    )
    out = pl.pallas_call(
        _combine_kernel,
        grid_spec=grid_spec,
        out_shape=out_shape,
        compiler_params=_cparams("arbitrary"),
        name="combine",
    )(*args)
    return (out[0], out[1]) if next_g is not None else (out[0], None)


def _route_plan(expert_ids, n_tokens):
    tm, tc = MOE_TILE, COMBINE_TILE
    n_assign = 2 * n_tokens
    n_tiles = n_assign // tm + N_EXPERTS
    e_flat = expert_ids.reshape(-1)
    onehot = (e_flat[:, None] == jnp.arange(N_EXPERTS, dtype=jnp.int32)[None, :]).astype(jnp.int32)
    csum = jnp.cumsum(onehot, axis=0)
    rank = jnp.sum(csum * onehot, axis=1) - 1
    counts = csum[-1]
    padded = ((counts + tm - 1) // tm) * tm
    ends = jnp.cumsum(padded)
    offs = ends - padded
    pos = jnp.sum(onehot * offs[None, :], axis=1) + rank
    row_token = jnp.zeros((n_tiles * tm,), jnp.int32).at[pos].set(jnp.arange(n_assign, dtype=jnp.int32) // 2)
    tile_start = jnp.arange(n_tiles, dtype=jnp.int32) * tm
    tile_expert = jnp.minimum(jnp.sum((ends[None, :] <= tile_start[:, None]).astype(jnp.int32), axis=1),
                              N_EXPERTS - 1)
    pos_tiles = pos.reshape(n_tokens // tc, tc, 2).transpose(0, 2, 1).reshape(-1)
    n_used = (ends[-1:] // tm).astype(jnp.int32)
    return tile_expert, row_token, n_used, pos_tiles


def kernel(x, norm1_g, w_in, hgrn_lb, hgrn_norm_g, mlstm_conv_w, mlstm_i_bias, mlstm_f_bias, mlstm_norm_g,
           diff_qn_g, diff_kn_g, diff_lam_q1, diff_lam_k1, diff_lam_q2, diff_lam_k2, diff_subln_g, w_branch_a,
           w_branch_b, w_branch_c, w_out, norm2_g, router_group_w, router_group_b, router_expert_w,
           router_expert_b, expert_w1, expert_w3, expert_w2):
    bsz, s_len, d = x.shape
    depth = w_in.shape[0]
    t = bsz * s_len
    a_w, b_w, c_w = A_HEADS * HEAD_DIM, B_HEADS * HEAD_DIM, C_HEADS * HEAD_DIM
    o_b = 4 * a_w
    o_s = o_b + 4 * b_w
    o_c = o_s + 2 * B_HEADS
    o_g = o_c + 3 * c_w

    tables = _rope_tables(s_len)
    p_lb = jax.nn.softmax(hgrn_lb.astype(F32), axis=0)
    lb_all = jnp.cumsum(p_lb, axis=0) - p_lb[0:1]

    w_c = w_in[:, :, o_c:o_g].astype(BF16)
    w_g = w_in[:, :, o_g:].astype(BF16)
    w_s = jnp.pad(w_in[:, :, o_s:o_c], ((0, 0), (0, 0), (0, LANES - 2 * B_HEADS))).astype(BF16)
    ff = expert_w1.shape[-1]
    ew1 = expert_w1.astype(BF16).reshape(depth * N_EXPERTS, d, ff)
    ew3 = expert_w3.astype(BF16).reshape(depth * N_EXPERTS, d, ff)
    ew2 = expert_w2.astype(BF16).reshape(depth * N_EXPERTS, ff, d)

    xf = x.reshape(t, d)
    xn = _rmsnorm(xf, norm1_g[0], BF16)
    for l in range(depth):
        h_a = _matmul(xn, w_in, l, BF16, col0=0, ncols=o_b, name="in_proj_a").reshape(bsz, s_len, -1)
        h_b = _matmul(xn, w_in, l, BF16, col0=o_b, ncols=o_s - o_b, name="in_proj_b").reshape(bsz, s_len, -1)
        h_c = _matmul(xn, w_c, l, BF16, name="in_proj_c").reshape(bsz, s_len, -1)
        gates = _matmul(xn, w_g, l, BF16, name="in_proj_gates")
        h_s = _matmul(xn, w_s, l, F32, name="in_proj_scalar_gates").reshape(bsz, s_len, LANES)
        gate_i = h_s[..., :B_HEADS] + mlstm_i_bias[l]
        gate_f = h_s[..., B_HEADS:2 * B_HEADS] + mlstm_f_bias[l]

        out_a = _hgrn2(h_a, lb_all[l], hgrn_norm_g[l])
        out_b = _mlstm(h_b, mlstm_conv_w[l], gate_i, gate_f, mlstm_norm_g[l])
        lambda_init = 0.8 - 0.6 * math.exp(-0.3 * l)
        lam_vecs = jnp.stack([diff_lam_q1[l], diff_lam_k1[l], diff_lam_q2[l], diff_lam_k2[l]]).astype(F32)
        out_c = _diff_attn(h_c, tables, diff_qn_g[l], diff_kn_g[l], lam_vecs, diff_subln_g[l], lambda_init)

        y = _merge(out_a.reshape(t, a_w), out_b.reshape(t, b_w), out_c.reshape(t, c_w),
                   w_branch_a, w_branch_b, w_branch_c, l, gates)
        xf = _matmul(y, w_out, l, F32, residual=xf, name="out_proj")

        n_router = N_GROUPS + N_EXPERTS
        w_router = jnp.pad(jnp.concatenate([router_group_w[l], router_expert_w[l]], axis=1),
                           ((0, 0), (0, LANES - n_router)))
        b_router = jnp.pad(jnp.concatenate([router_group_b[l], router_expert_b[l]]),
                           (0, LANES - n_router)).reshape(1, LANES)
        xq, route = _router(xf, norm2_g[l], w_router, b_router)
        tile_expert, row_token, n_used, pos_tiles = _route_plan(route[:, 0:2].astype(jnp.int32), t)
        y_sorted = _experts(xq, ew1, ew3, ew2, l, tile_expert, row_token, n_used)
        xf, xn = _combine(y_sorted, xf, route, pos_tiles, norm1_g[l + 1] if l + 1 < depth else None)
    return xf.reshape(bsz, s_len, d)
```

```python
import functools
import math

import jax
import jax.numpy as jnp
from jax import lax
from jax.experimental import pallas as pl
from jax.experimental.pallas import tpu as pltpu

F32 = jnp.float32
BF16 = jnp.bfloat16
FP8 = jnp.float8_e4m3fn
FP8_MAX = 448.0
FP8_TARGET = 224.0

HEAD_DIM = 128
A_HEADS = 8
B_HEADS = 12
C_HEADS = 12
C_DQK = 64
CONV_K = 4
ROPE_THETA = 500000.0
ROPE_DIM = C_DQK // 4
ROPE_HALF = ROPE_DIM // 2
N_GROUPS = 4
EXPERTS_PER_GROUP = 8
N_EXPERTS = N_GROUPS * EXPERTS_PER_GROUP
EPS = 1e-6
LANES = 128
NEG_BIG = -1e30

VMEM_LIMIT = 56 * 1024 * 1024

HGRN_CHUNK = 128
HGRN_SUB = 32
HGRN_EXP_CLAMP = 60.0
MLSTM_CHUNK = 128
ATT_BLOCK = 256
MASK_CHUNK = 64
MOE_TILE = 256
COMBINE_TILE = 128
GATHER_UNROLL = 8


def _cparams(*sem):
    return pltpu.CompilerParams(dimension_semantics=sem, vmem_limit_bytes=VMEM_LIMIT)


def _sigmoid(x):
    return 1.0 / (1.0 + jnp.exp(-x))


def _dot(a, b):
    return jnp.dot(a, b, preferred_element_type=F32)


def _dot_nt(a, b):
    return lax.dot_general(a, b, (((1,), (1,)), ((), ())), preferred_element_type=F32)


def _dot_tn(a, b):
    return lax.dot_general(a, b, (((0,), (0,)), ((), ())), preferred_element_type=F32)


def _split_dot(exact_lhs, x):
    hi = x.astype(BF16)
    lo = (x - hi.astype(F32)).astype(BF16)
    return _dot(exact_lhs, hi) + _dot(exact_lhs, lo)


def _split_dot_rhs(x, exact_rhs):
    hi = x.astype(BF16)
    lo = (x - hi.astype(F32)).astype(BF16)
    return _dot(hi, exact_rhs) + _dot(lo, exact_rhs)


def _to_fp8(y):
    return jnp.clip(y, -FP8_MAX, FP8_MAX).astype(FP8)


def _rmsnorm_kernel(x_ref, g_ref, o_ref, o8_ref):
    x = x_ref[...]
    y = x * lax.rsqrt(jnp.mean(x * x, axis=-1, keepdims=True) + EPS) * g_ref[...]
    o_ref[...] = y.astype(o_ref.dtype)
    o8_ref[...] = _to_fp8(y)


def _rmsnorm(x, g, tm=256):
    t, d = x.shape
    tile = pl.BlockSpec((tm, d), lambda i: (i, 0))
    return pl.pallas_call(
        _rmsnorm_kernel,
        grid=(t // tm,),
        in_specs=[tile, pl.BlockSpec((1, d), lambda i: (0, 0))],
        out_specs=[tile, tile],
        out_shape=[jax.ShapeDtypeStruct((t, d), BF16), jax.ShapeDtypeStruct((t, d), FP8)],
        compiler_params=_cparams("parallel"),
        name="rmsnorm",
    )(x, g.reshape(1, d))


def _mm_kernel(x_ref, w_ref, *rest, has_residual, cast_weight):
    r_ref = rest[0] if has_residual else None
    o_ref = rest[1] if has_residual else rest[0]
    if cast_weight:
        wb_ref = rest[-1]

        @pl.when(pl.program_id(1) == 0)
        def _():
            wb_ref[...] = w_ref[0].astype(BF16)

        acc = _dot(x_ref[...], wb_ref[...])
    else:
        acc = _dot(x_ref[...], w_ref[0])
    if has_residual:
        acc = acc + r_ref[...]
    o_ref[...] = acc.astype(o_ref.dtype)


def _pick_tile(n, candidates):
    for c in candidates:
        if n % c == 0:
            return c
    return n


def _matmul(x, w, layer, out_dtype, col0=0, ncols=None, residual=None, name="matmul"):
    m, k = x.shape
    n = w.shape[2] - col0 if ncols is None else ncols
    cast_weight = w.dtype != BF16
    tm = _pick_tile(m, (1024, 512, 256, 128))
    tn = _pick_tile(math.gcd(n, col0) if col0 else n, (512,) if cast_weight else (1024, 768, 512, 256, 128))
    cb = col0 // tn
    in_specs = [pl.BlockSpec((tm, k), lambda j, i: (i, 0)),
                pl.BlockSpec((1, k, tn), lambda j, i: (layer, 0, cb + j))]
    args = [x, w]
    if residual is not None:
        in_specs.append(pl.BlockSpec((tm, tn), lambda j, i: (i, j)))
        args.append(residual)
    return pl.pallas_call(
        functools.partial(_mm_kernel, has_residual=residual is not None, cast_weight=cast_weight),
        grid=(n // tn, m // tm),
        in_specs=in_specs,
        out_specs=pl.BlockSpec((tm, tn), lambda j, i: (i, j)),
        out_shape=jax.ShapeDtypeStruct((m, n), out_dtype),
        scratch_shapes=[pltpu.VMEM((k, tn), BF16)] if cast_weight else [],
        compiler_params=_cparams("parallel", "arbitrary"),
        name=name,
    )(*args)


WT_TILE = 512
WT_EDGE = 32


def _mm_wt_kernel(x_ref, a_ref, *rest, shift, quantize):
    b_ref = rest[0] if shift else None
    rest = rest[1:] if shift else rest
    o_ref, wb_ref = rest[0], rest[1]
    inv_ref = rest[2] if quantize else None

    @pl.when(pl.program_id(1) == 0)
    def _():
        if shift:
            tn = a_ref.shape[1]
            w = jnp.concatenate([a_ref[0, shift:tn, :], b_ref[0, 0:shift, :]], axis=0)
        else:
            w = a_ref[0]
        if quantize:
            top = jnp.max(jnp.max(jnp.abs(w), axis=1, keepdims=True), axis=0, keepdims=True)
            top = jnp.maximum(top, 1e-30)
            wb_ref[...] = (w * (FP8_TARGET / top)).astype(FP8)
            inv_ref[...] = top * (1.0 / FP8_TARGET)
        else:
            wb_ref[...] = w.astype(BF16)

    acc = _dot_nt(x_ref[...], wb_ref[...])
    if quantize:
        acc = acc * inv_ref[...]
    o_ref[...] = acc.astype(o_ref.dtype)


def _matmul_wt(x, wt, layer, out_dtype, row0, nrows, name, quantize=False):
    m, k = x.shape
    tn = _pick_tile(nrows, (WT_TILE, LANES))
    shift = row0 % tn
    base = row0 - shift
    assert shift % 8 == 0 and shift <= WT_EDGE and tn % WT_EDGE == 0 and base % tn == 0
    tm = _pick_tile(m, (1024, 512, 256, 128))
    cb = base // tn
    in_specs = [pl.BlockSpec((tm, k), lambda j, i: (i, 0)),
                pl.BlockSpec((1, tn, k), lambda j, i: (layer, cb + j, 0))]
    args = [x, wt]
    if shift:
        per = tn // WT_EDGE
        in_specs.append(pl.BlockSpec((1, WT_EDGE, k), lambda j, i: (layer, (cb + j + 1) * per, 0)))
        args.append(wt)
    return pl.pallas_call(
        functools.partial(_mm_wt_kernel, shift=shift, quantize=quantize),
        grid=(nrows // tn, m // tm),
        in_specs=in_specs,
        out_specs=pl.BlockSpec((tm, tn), lambda j, i: (i, j)),
        out_shape=jax.ShapeDtypeStruct((m, nrows), out_dtype),
        scratch_shapes=([pltpu.VMEM((tn, k), FP8), pltpu.VMEM((1, 1), F32)] if quantize
                        else [pltpu.VMEM((tn, k), BF16)]),
        compiler_params=_cparams("parallel", "arbitrary"),
        name=name,
    )(*args)


def _hgrn2_head(qp, z, v, gp, lb, ng, st):
    ch, sb = HGRN_CHUNK, HGRN_SUB
    nsb = ch // sb
    sig = _sigmoid(z)
    logf = jnp.log(lb + (1.0 - lb) * sig)
    k = (1.0 - lb) * (1.0 - sig)
    q = qp * _sigmoid(qp)

    row = lax.broadcasted_iota(jnp.int32, (ch, ch), 0)
    col = lax.broadcasted_iota(jnp.int32, (ch, ch), 1)
    causal = row >= col
    tri = jnp.where(causal, 1.0, 0.0).astype(BF16)
    b = _split_dot(tri, logf)
    b_last = b[ch - 1:ch, :]

    o = _dot_nt((q * jnp.exp(b)).astype(BF16), st.astype(BF16))

    refs = [b[i * sb:i * sb + 1, :] for i in range(nsb)]
    rblk = jnp.concatenate([jnp.broadcast_to(r, (sb, HEAD_DIM)) for r in refs], axis=0)
    qt = q * jnp.exp(b - rblk)
    rowblk = lax.broadcasted_iota(jnp.int32, (ch, HEAD_DIM), 0) // sb
    qcat = jnp.concatenate([jnp.where(rowblk == i, qt, 0.0).astype(BF16) for i in range(nsb)], axis=1)
    kcat = jnp.concatenate(
        [(k * jnp.exp(jnp.minimum(r - b, HGRN_EXP_CLAMP))).astype(BF16) for r in refs], axis=1)
    s = jnp.where(causal, _dot_nt(qcat, kcat), 0.0)
    o = o + _dot(s.astype(BF16), v)

    kd = (k * jnp.exp(b_last - b)).astype(BF16)
    st_new = st * jnp.exp(b_last) + _dot_tn(v, kd)

    y = o * lax.rsqrt(jnp.mean(o * o, axis=-1, keepdims=True) + EPS) * ng
    return y * (gp * _sigmoid(gp)), st_new


def _hgrn2_kernel(q_ref, f_ref, i_ref, g_ref, lb_ref, ng_ref, o_ref, st_ref):
    @pl.when(pl.program_id(1) == 0)
    def _():
        st_ref[...] = jnp.zeros_like(st_ref)

    heads = range(A_HEADS)
    state = [st_ref[h] for h in heads]
    new_state = []
    for h in heads:
        cols = slice(h * HEAD_DIM, (h + 1) * HEAD_DIM)
        y, st = _hgrn2_head(q_ref[0, :, cols].astype(F32), f_ref[0, :, cols].astype(F32), i_ref[0, :, cols],
                            g_ref[0, :, cols].astype(F32), lb_ref[:, cols], ng_ref[...], state[h])
        new_state.append(st)
        o_ref[0, :, cols] = y.astype(o_ref.dtype)
    for h in heads:
        st_ref[h] = new_state[h]


def _hgrn2(h_a, lb, norm_g):
    bsz, s_len, _ = h_a.shape
    ch = HGRN_CHUNK
    a_w = A_HEADS * HEAD_DIM

    def col(j):
        return pl.BlockSpec((1, ch, a_w), lambda b, c, j=j: (b, c, j))

    return pl.pallas_call(
        _hgrn2_kernel,
        grid=(bsz, s_len // ch),
        in_specs=[col(0), col(1), col(2), col(3),
                  pl.BlockSpec((1, a_w), lambda b, c: (0, 0)),
                  pl.BlockSpec((1, HEAD_DIM), lambda b, c: (0, 0))],
        out_specs=pl.BlockSpec((1, ch, a_w), lambda b, c: (b, c, 0)),
        out_shape=jax.ShapeDtypeStruct((bsz, s_len, a_w), BF16),
        scratch_shapes=[pltpu.VMEM((A_HEADS, HEAD_DIM, HEAD_DIM), F32)],
        compiler_params=_cparams("parallel", "arbitrary"),
        name="hgrn2",
    )(h_a, h_a, h_a, h_a, lb.reshape(1, -1), norm_g.reshape(1, HEAD_DIM))


def _log_sigmoid(x):
    return jnp.minimum(x, 0.0) - jnp.log(1.0 + jnp.exp(-jnp.abs(x)))


def _causal_conv_silu(x, prev, w):
    rowi = lax.broadcasted_iota(jnp.int32, x.shape, 0)
    acc = x * w[CONV_K - 1:CONV_K, :]
    for sft in range(1, CONV_K):
        shifted = jnp.where(rowi >= sft, pltpu.roll(x, sft, 0), pltpu.roll(prev, sft, 0))
        acc = acc + shifted * w[CONV_K - 1 - sft:CONV_K - sft, :]
    return acc * _sigmoid(acc)


def _mlstm_kernel(q_ref, k_ref, v_ref, og_ref, w_ref, gir_ref, gfr_ref, gic_ref, gfc_ref, ng_ref,
                  o_ref, c_ref, n_ref, m_ref, qprev_ref, kprev_ref):
    @pl.when(pl.program_id(1) == 0)
    def _():
        c_ref[...] = jnp.zeros_like(c_ref)
        n_ref[...] = jnp.zeros_like(n_ref)
        m_ref[...] = jnp.zeros_like(m_ref)
        qprev_ref[...] = jnp.zeros_like(qprev_ref)
        kprev_ref[...] = jnp.zeros_like(kprev_ref)

    ch = MLSTM_CHUNK
    b_w = B_HEADS * HEAD_DIM
    heads = range(B_HEADS)
    row = lax.broadcasted_iota(jnp.int32, (ch, ch), 0)
    col = lax.broadcasted_iota(jnp.int32, (ch, ch), 1)
    tril = row >= col
    triu = row <= col

    def cols(h):
        return slice(h * HEAD_DIM, (h + 1) * HEAD_DIM)

    qprev = qprev_ref[...]
    kprev = kprev_ref[...]
    qc, kc, g, inter, logw_c, logw_r, m_prev, f_last = [], [], [], [], [], [], [], []
    for h in heads:
        qc.append(_causal_conv_silu(q_ref[0, :, cols(h)].astype(F32), qprev[:, cols(h)], w_ref[:, cols(h)])
                  * (HEAD_DIM ** -0.5))
        kc.append(_causal_conv_silu(k_ref[0, :, cols(h)].astype(F32), kprev[:, cols(h)],
                                    w_ref[:, b_w + h * HEAD_DIM:b_w + (h + 1) * HEAD_DIM]))
        li_r = gir_ref[0, 0, h:h + 1, :]
        li_c = gic_ref[0, :, h:h + 1]
        lf_r = _log_sigmoid(gfr_ref[0, 0, h:h + 1, :])
        lf_c = _log_sigmoid(gfc_ref[0, :, h:h + 1])
        fc_c = jnp.sum(jnp.where(tril, lf_r, 0.0), axis=1, keepdims=True)
        fc_r = jnp.sum(jnp.where(triu, lf_c, 0.0), axis=0, keepdims=True)
        g.append(jnp.where(tril, fc_c - fc_r + li_r, NEG_BIG))
        m_prev.append(m_ref[h])
        inter.append(fc_c + m_prev[h])
        f_last.append(fc_c[ch - 1:ch, :])
        logw_c.append(f_last[h] - fc_c + li_c)
        logw_r.append(f_last[h] - fc_r + li_r)
    qprev_ref[...] = q_ref[0].astype(F32)
    kprev_ref[...] = k_ref[0].astype(F32)

    m, num, den, m_new = [], [], [], []
    for h in heads:
        m.append(jnp.maximum(jnp.max(g[h], axis=1, keepdims=True), inter[h]))
        m_new.append(jnp.maximum(f_last[h] + m_prev[h], jnp.max(logw_r[h], axis=1, keepdims=True)))
    for h in heads:
        dmat = jnp.exp(g[h] - m[h])
        a = jnp.exp(inter[h] - m[h])
        qb = qc[h].astype(BF16)
        sc = _dot_nt(qb, kc[h].astype(BF16)) * dmat
        num.append(_dot(sc.astype(BF16), v_ref[0, :, cols(h)]) + a * _dot(qb, c_ref[h].astype(BF16)))
        den.append(jnp.sum(sc, axis=1, keepdims=True) + a * jnp.sum(qc[h] * n_ref[h], axis=1, keepdims=True))

    for h in heads:
        dec = jnp.exp(f_last[h] + m_prev[h] - m_new[h])
        kw = kc[h] * jnp.exp(logw_c[h] - m_new[h])
        c_ref[h] = dec * c_ref[h] + _dot_tn(kw.astype(BF16), v_ref[0, :, cols(h)])
        n_ref[h] = dec * n_ref[h] + jnp.sum(kw, axis=0, keepdims=True)
        m_ref[h] = m_new[h]

    hs = [num[h] / jnp.maximum(jnp.abs(den[h]), jnp.exp(-m[h])) for h in heads]
    ms = [jnp.mean(hs[h] * hs[h], axis=-1, keepdims=True) for h in heads]
    for h in heads:
        y = hs[h] * lax.rsqrt(ms[h] + EPS) * ng_ref[...]
        o_ref[0, :, cols(h)] = (y * _sigmoid(og_ref[0, :, cols(h)].astype(F32))).astype(o_ref.dtype)


def _mlstm(h_b, conv_w, gate_i, gate_f, norm_g):
    bsz, s_len, _ = h_b.shape
    ch = MLSTM_CHUNK
    nc = s_len // ch
    b_w = B_HEADS * HEAD_DIM

    def col(j):
        return pl.BlockSpec((1, ch, b_w), lambda b, c, j=j: (b, c, j))

    def rows(gt):
        return gt.reshape(bsz, nc, ch, B_HEADS).transpose(0, 1, 3, 2)

    row_spec = pl.BlockSpec((1, 1, B_HEADS, ch), lambda b, c: (b, c, 0, 0))
    col_spec = pl.BlockSpec((1, ch, B_HEADS), lambda b, c: (b, c, 0))
    return pl.pallas_call(
        _mlstm_kernel,
        grid=(bsz, nc),
        in_specs=[col(0), col(1), col(2), col(3),
                  pl.BlockSpec((CONV_K, 2 * b_w), lambda b, c: (0, 0)),
                  row_spec, row_spec, col_spec, col_spec,
                  pl.BlockSpec((1, HEAD_DIM), lambda b, c: (0, 0))],
        out_specs=pl.BlockSpec((1, ch, b_w), lambda b, c: (b, c, 0)),
        out_shape=jax.ShapeDtypeStruct((bsz, s_len, b_w), BF16),
        scratch_shapes=[pltpu.VMEM((B_HEADS, HEAD_DIM, HEAD_DIM), F32), pltpu.VMEM((B_HEADS, 1, HEAD_DIM), F32),
                        pltpu.VMEM((B_HEADS, 1, 1), F32), pltpu.VMEM((ch, b_w), F32),
                        pltpu.VMEM((ch, b_w), F32)],
        compiler_params=_cparams("parallel", "arbitrary"),
        name="mlstm",
    )(h_b, h_b, h_b, h_b, conv_w, rows(gate_i), rows(gate_f), gate_i, gate_f, norm_g.reshape(1, HEAD_DIM))


def _diff_attn_kernel(q_ref, k_ref, v_ref, cos_ref, sa_ref, sb_ref, qg_ref, kg_ref, lam_ref, sg_ref,
                      o_ref, qs_ref, kp_ref, vx_ref, *, lambda_init):
    s_len = q_ref.shape[1]
    tq = ATT_BLOCK
    nq = s_len // tq
    lane = lax.broadcasted_iota(jnp.int32, (tq, LANES), 1)
    first_map = lane < C_DQK
    li = lax.broadcasted_iota(jnp.int32, (LANES, LANES), 0) // C_DQK
    lj = lax.broadcasted_iota(jnp.int32, (LANES, LANES), 1) // C_DQK
    same_map = jnp.where(li == lj, 1.0, 0.0).astype(BF16)

    def qk_norm_rope(x, g, rows):
        ss = _split_dot_rhs(x * x, same_map)
        y = x * lax.rsqrt(ss * (1.0 / C_DQK) + EPS) * g
        return (y * cos_ref[rows, :] + pltpu.roll(y, LANES - ROPE_HALF, 1) * sa_ref[rows, :]
                + pltpu.roll(y, ROPE_HALF, 1) * sb_ref[rows, :])

    def prep(i, carry):
        rows = pl.ds(pl.multiple_of(i * tq, tq), tq)
        qn = qk_norm_rope(q_ref[0, rows, :].astype(F32), qg_ref[...], rows) * (C_DQK ** -0.5)
        qs_ref[i, 0:tq, :] = jnp.where(first_map, qn, 0.0).astype(BF16)
        qs_ref[i, tq:2 * tq, :] = jnp.where(first_map, 0.0, qn).astype(BF16)
        kp_ref[rows, :] = qk_norm_rope(k_ref[0, rows, :].astype(F32), kg_ref[...], rows).astype(BF16)
        vx_ref[rows, 0:HEAD_DIM] = v_ref[0, rows, :]
        vx_ref[rows, HEAD_DIM:2 * HEAD_DIM] = jnp.ones((tq, HEAD_DIM), BF16)
        return carry

    lax.fori_loop(0, nq, prep, 0)

    lamv = lam_ref[...]
    lam = (jnp.exp(jnp.sum(lamv[0:1] * lamv[1:2], axis=1, keepdims=True))
           - jnp.exp(jnp.sum(lamv[2:3] * lamv[3:4], axis=1, keepdims=True)) + lambda_init)
    qchunk = (lax.broadcasted_iota(jnp.int32, (2 * tq, tq), 0) % tq) // MASK_CHUNK
    kchunk = lax.broadcasted_iota(jnp.int32, (2 * tq, tq), 1) // MASK_CHUNK
    diag_ok = qchunk >= kchunk

    for i in range(nq):
        k0 = i * tq
        qs = qs_ref[i]
        s_d = jnp.where(diag_ok, _dot_nt(qs, kp_ref[k0:k0 + tq, :]), NEG_BIG)
        m = jnp.max(s_d, axis=1, keepdims=True)
        if i > 0:
            s_o = _dot_nt(qs, kp_ref[0:k0, :])
            m = jnp.maximum(m, jnp.max(s_o, axis=1, keepdims=True))
            acc = _dot(jnp.exp(s_o - m).astype(BF16), vx_ref[0:k0, :])
            acc = acc + _dot(jnp.exp(s_d - m).astype(BF16), vx_ref[k0:k0 + tq, :])
        else:
            acc = _dot(jnp.exp(s_d - m).astype(BF16), vx_ref[k0:k0 + tq, :])
        o = acc[:, 0:HEAD_DIM] / acc[:, HEAD_DIM:2 * HEAD_DIM]
        out = o[0:tq] - lam * o[tq:2 * tq]
        y = out * lax.rsqrt(jnp.mean(out * out, axis=-1, keepdims=True) + EPS) * sg_ref[...]
        o_ref[0, k0:k0 + tq, :] = (y * (1.0 - lambda_init)).astype(o_ref.dtype)


def _rope_tables(s_len):
    pos = jnp.arange(s_len, dtype=F32)
    inv_freq = ROPE_THETA ** (-jnp.arange(0, ROPE_DIM, 2, dtype=F32) / ROPE_DIM)
    ang = pos[:, None] * inv_freq[None, :]
    cos, sin = jnp.cos(ang), jnp.sin(ang)
    ones = jnp.ones((s_len, C_DQK - ROPE_DIM), F32)
    zeros = jnp.zeros((s_len, C_DQK - ROPE_DIM), F32)
    zh = jnp.zeros((s_len, ROPE_HALF), F32)
    cos_map = jnp.concatenate([cos, cos, ones], axis=1)
    sa_map = jnp.concatenate([-sin, zh, zeros], axis=1)
    sb_map = jnp.concatenate([zh, sin, zeros], axis=1)
    return tuple(jnp.concatenate([t, t], axis=1) for t in (cos_map, sa_map, sb_map))


def _diff_attn(h_c, tables, qn_g, kn_g, lam_vecs, subln_g, lambda_init):
    bsz, s_len, _ = h_c.shape

    def col(j):
        return pl.BlockSpec((1, s_len, HEAD_DIM), lambda b, h, j=j: (b, 0, j * C_HEADS + h))

    def full(shape):
        return pl.BlockSpec(shape, lambda b, h: (0,) * len(shape))

    return pl.pallas_call(
        functools.partial(_diff_attn_kernel, lambda_init=lambda_init),
        grid=(bsz, C_HEADS),
        in_specs=[col(0), col(1), col(2), full((s_len, LANES)), full((s_len, LANES)), full((s_len, LANES)),
                  full((1, LANES)), full((1, LANES)), full((4, C_DQK)), full((1, HEAD_DIM))],
        out_specs=pl.BlockSpec((1, s_len, HEAD_DIM), lambda b, h: (b, 0, h)),
        out_shape=jax.ShapeDtypeStruct((bsz, s_len, C_HEADS * HEAD_DIM), BF16),
        scratch_shapes=[pltpu.VMEM((s_len // ATT_BLOCK, 2 * ATT_BLOCK, LANES), BF16),
                        pltpu.VMEM((s_len, LANES), BF16), pltpu.VMEM((s_len, 2 * HEAD_DIM), BF16)],
        compiler_params=_cparams("parallel", "parallel"),
        name="diff_attn",
    )(h_c, h_c, h_c, *tables, jnp.tile(qn_g, 2).reshape(1, LANES), jnp.tile(kn_g, 2).reshape(1, LANES),
      lam_vecs, subln_g.reshape(1, HEAD_DIM))


def _merge_kernel(oa_ref, ob_ref, oc_ref, wa_ref, wb_ref, wc_ref, ga_ref, gb_ref, gc_ref, y_ref,
                  wab_ref, wbb_ref, wcb_ref):
    @pl.when(pl.program_id(1) == 0)
    def _():
        wab_ref[...] = wa_ref[0].astype(BF16)
        wbb_ref[...] = wb_ref[0].astype(BF16)
        wcb_ref[...] = wc_ref[0].astype(BF16)

    y = _sigmoid(ga_ref[...].astype(F32)) * _dot(oa_ref[...], wab_ref[...])
    y = y + _sigmoid(gb_ref[...].astype(F32)) * _dot(ob_ref[...], wbb_ref[...])
    y = y + _sigmoid(gc_ref[...].astype(F32)) * _dot(oc_ref[...], wcb_ref[...])
    y_ref[...] = y.astype(y_ref.dtype)


def _merge(oa, ob, oc, wa, wb, wc, layer, gates):
    m = oa.shape[0]
    n = wa.shape[2]
    tm = _pick_tile(m, (1024, 512, 256, 128))
    tn = _pick_tile(n, (512, 256, 128))
    nb = n // tn

    def lhs(a):
        return pl.BlockSpec((tm, a.shape[1]), lambda j, i: (i, 0))

    def rhs(w):
        return pl.BlockSpec((1, w.shape[1], tn), lambda j, i: (layer, 0, j))

    def gate(k):
        return pl.BlockSpec((tm, tn), lambda j, i, k=k: (i, k * nb + j))

    return pl.pallas_call(
        _merge_kernel,
        grid=(nb, m // tm),
        in_specs=[lhs(oa), lhs(ob), lhs(oc), rhs(wa), rhs(wb), rhs(wc), gate(0), gate(1), gate(2)],
        out_specs=pl.BlockSpec((tm, tn), lambda j, i: (i, j)),
        out_shape=jax.ShapeDtypeStruct((m, n), BF16),
        scratch_shapes=[pltpu.VMEM((w.shape[1], tn), BF16) for w in (wa, wb, wc)],
        compiler_params=_cparams("parallel", "arbitrary"),
        name="merge",
    )(oa, ob, oc, wa, wb, wc, gates, gates, gates)


def _pack_bf16_pairs(x):
    n = x.shape[1] // 2
    lo = lax.bitcast_convert_type(x[:, :n].astype(BF16).astype(F32), jnp.uint32)
    hi = lax.bitcast_convert_type(x[:, n:].astype(BF16).astype(F32), jnp.uint32)
    return (hi & jnp.uint32(0xFFFF0000)) | (lo >> 16)


def _unpack_bf16_pairs(u):
    lo = lax.bitcast_convert_type(u << 16, F32)
    hi = lax.bitcast_convert_type(u & jnp.uint32(0xFFFF0000), F32)
    return lo, hi


def _router_kernel(x_ref, g_ref, w_ref, b_ref, xq_ref, route_ref):
    x = x_ref[...]
    xn = x * lax.rsqrt(jnp.mean(x * x, axis=-1, keepdims=True) + EPS) * g_ref[...]
    xq_ref[...] = _pack_bf16_pairs(xn)
    logits = jnp.dot(xn, w_ref[...], preferred_element_type=F32, precision=lax.Precision.HIGHEST) + b_ref[...]
    lane = lax.broadcasted_iota(jnp.int32, logits.shape, 1).astype(F32)
    none = float(LANES)

    def first_argmax(vals):
        top = jnp.max(vals, axis=1, keepdims=True)
        return top, jnp.min(jnp.where(vals == top, lane, none), axis=1, keepdims=True)

    is_group = lane < N_GROUPS
    g_top, g_idx = first_argmax(jnp.where(is_group, logits, NEG_BIG))
    g_w = 1.0 / jnp.sum(jnp.where(is_group, jnp.exp(logits - g_top), 0.0), axis=1, keepdims=True)
    lo = N_GROUPS + EXPERTS_PER_GROUP * g_idx
    e_logits = jnp.where((lane >= lo) & (lane < lo + EXPERTS_PER_GROUP), logits, NEG_BIG)
    v1, i1 = first_argmax(e_logits)
    v2, i2 = first_argmax(jnp.where(lane == i1, NEG_BIG, e_logits))
    e2 = jnp.exp(v2 - v1)
    w1 = g_w / (1.0 + e2)
    w2 = w1 * e2
    route = jnp.where(lane == 0, i1 - N_GROUPS, jnp.where(lane == 1, i2 - N_GROUPS,
                      jnp.where(lane == 2, w1, jnp.where(lane == 3, w2, 0.0))))
    route_ref[...] = route


def _router(x, g, w_router, b_router, tm=256):
    t, d = x.shape
    return pl.pallas_call(
        _router_kernel,
        grid=(t // tm,),
        in_specs=[pl.BlockSpec((tm, d), lambda i: (i, 0)), pl.BlockSpec((1, d), lambda i: (0, 0)),
                  pl.BlockSpec((d, LANES), lambda i: (0, 0)), pl.BlockSpec((1, LANES), lambda i: (0, 0))],
        out_specs=[pl.BlockSpec((tm, d // 2), lambda i: (i, 0)), pl.BlockSpec((tm, LANES), lambda i: (i, 0))],
        out_shape=[jax.ShapeDtypeStruct((t, d // 2), jnp.uint32), jax.ShapeDtypeStruct((t, LANES), F32)],
        compiler_params=_cparams("parallel"),
        name="router",
    )(x, g.reshape(1, d), w_router, b_router)


def _gather_rows(src_hbm, dst_ref, idx_ref, base, n_rows, sem):
    def body(j, carry):
        for u in range(GATHER_UNROLL):
            r = j * GATHER_UNROLL + u
            pltpu.make_async_copy(src_hbm.at[pl.ds(idx_ref[base + r], 1)], dst_ref.at[pl.ds(r, 1)],
                                  sem).start(priority=u % 2)
        return carry
    lax.fori_loop(0, n_rows // GATHER_UNROLL, body, 0)


def _wait_rows(src_hbm, dst_ref, n_rows, sem):
    pltpu.make_async_copy(src_hbm.at[pl.ds(0, n_rows)], dst_ref, sem).wait()


def _expert_kernel(tile_expert_ref, row_token_ref, n_used_ref, x_hbm, w1_ref, w3_ref, w2_ref, y_ref, xbuf, sem):
    del tile_expert_ref
    i = pl.program_id(0)
    n_used = n_used_ref[0]
    tm = MOE_TILE
    half = w1_ref.shape[1] // 2
    slot = i % 2

    @pl.when((i == 0) & (n_used > 0))
    def _():
        _gather_rows(x_hbm, xbuf.at[0], row_token_ref, 0, tm, sem.at[0])

    @pl.when(i + 1 < n_used)
    def _():
        _gather_rows(x_hbm, xbuf.at[1 - slot], row_token_ref, (i + 1) * tm, tm, sem.at[1 - slot])

    @pl.when(i < n_used)
    def _():
        _wait_rows(x_hbm, xbuf.at[slot], tm, sem.at[slot])
        lo, hi = _unpack_bf16_pairs(xbuf[slot])
        lo = lo.astype(BF16)
        hi = hi.astype(BF16)
        h1 = _dot(lo, w1_ref[0, 0:half, :]) + _dot(hi, w1_ref[0, half:2 * half, :])
        h3 = _dot(lo, w3_ref[0, 0:half, :]) + _dot(hi, w3_ref[0, half:2 * half, :])
        h = (h1 * _sigmoid(h1) * h3).astype(BF16)
        y_ref[...] = _pack_bf16_pairs(_dot(h, w2_ref[0]))

    @pl.when(i >= n_used)
    def _():
        y_ref[...] = jnp.zeros_like(y_ref)


def _experts(xq, w1, w3, w2, layer, tile_expert, row_token, n_used):
    t, dh = xq.shape
    d = 2 * dh
    ff = w1.shape[2]
    tm = MOE_TILE
    n_tiles = tile_expert.shape[0]
    e0 = layer * N_EXPERTS
    grid_spec = pltpu.PrefetchScalarGridSpec(
        num_scalar_prefetch=3,
        grid=(n_tiles,),
        in_specs=[pl.BlockSpec(memory_space=pl.ANY),
                  pl.BlockSpec((1, d, ff), lambda i, te, rt, nu: (e0 + te[i], 0, 0)),
                  pl.BlockSpec((1, d, ff), lambda i, te, rt, nu: (e0 + te[i], 0, 0)),
                  pl.BlockSpec((1, ff, d), lambda i, te, rt, nu: (e0 + te[i], 0, 0))],
        out_specs=pl.BlockSpec((tm, dh), lambda i, te, rt, nu: (i, 0)),
        scratch_shapes=[pltpu.VMEM((2, tm, dh), jnp.uint32), pltpu.SemaphoreType.DMA((2,))],
    )
    return pl.pallas_call(
        _expert_kernel,
        grid_spec=grid_spec,
        out_shape=jax.ShapeDtypeStruct((n_tiles * tm, dh), jnp.uint32),
        compiler_params=_cparams("arbitrary"),
        name="experts",
    )(tile_expert, row_token, n_used, xq, w1, w3, w2)


def _combine_kernel(pos_ref, y_hbm, x_ref, route_ref, *rest):
    if len(rest) == 6:
        g_ref, xo_ref, xn_ref, x8_ref, ybuf, sem = rest
    else:
        g_ref, xn_ref, x8_ref = None, None, None
        xo_ref, ybuf, sem = rest
    i = pl.program_id(0)
    n = pl.num_programs(0)
    tc = COMBINE_TILE
    slot = i % 2

    @pl.when(i == 0)
    def _():
        _gather_rows(y_hbm, ybuf.at[0], pos_ref, 0, 2 * tc, sem.at[0])

    @pl.when(i + 1 < n)
    def _():
        _gather_rows(y_hbm, ybuf.at[1 - slot], pos_ref, (i + 1) * 2 * tc, 2 * tc, sem.at[1 - slot])

    _wait_rows(y_hbm, ybuf.at[slot], 2 * tc, sem.at[slot])
    lo0, hi0 = _unpack_bf16_pairs(ybuf[slot, 0:tc, :])
    lo1, hi1 = _unpack_bf16_pairs(ybuf[slot, tc:2 * tc, :])
    w0 = route_ref[:, 2:3]
    w1 = route_ref[:, 3:4]
    x = x_ref[...] + jnp.concatenate([w0 * lo0 + w1 * lo1, w0 * hi0 + w1 * hi1], axis=1)
    xo_ref[...] = x
    if xn_ref is not None:
        y = x * lax.rsqrt(jnp.mean(x * x, axis=-1, keepdims=True) + EPS) * g_ref[...]
        xn_ref[...] = y.astype(xn_ref.dtype)
        x8_ref[...] = _to_fp8(y)


def _combine(y_sorted, x, route, pos_tiles, next_g=None):
    t, d = x.shape
    tc = COMBINE_TILE
    tile = pl.BlockSpec((tc, d), lambda i, p: (i, 0))
    in_specs = [pl.BlockSpec(memory_space=pl.ANY), tile, pl.BlockSpec((tc, LANES), lambda i, p: (i, 0))]
    out_specs = [tile]
    out_shape = [jax.ShapeDtypeStruct((t, d), F32)]
    args = [pos_tiles, y_sorted, x, route]
    if next_g is not None:
        in_specs.append(pl.BlockSpec((1, d), lambda i, p: (0, 0)))
        out_specs += [tile, tile]
        out_shape += [jax.ShapeDtypeStruct((t, d), BF16), jax.ShapeDtypeStruct((t, d), FP8)]
        args.append(next_g.reshape(1, d))
    grid_spec = pltpu.PrefetchScalarGridSpec(
        num_scalar_prefetch=1,
        grid=(t // tc,),
        in_specs=in_specs,
        out_specs=out_specs,
        scratch_shapes=[pltpu.VMEM((2, 2 * tc, d // 2), jnp.uint32), pltpu.SemaphoreType.DMA((2,))],
    )
    out = pl.pallas_call(
        _combine_kernel,
        grid_spec=grid_spec,
        out_shape=out_shape,
        compiler_params=_cparams("arbitrary"),
        name="combine",
    )(*args)
    return tuple(out) if next_g is not None else (out[0], None, None)


def _route_plan(expert_ids, n_tokens):
    tm, tc = MOE_TILE, COMBINE_TILE
    n_assign = 2 * n_tokens
    n_tiles = n_assign // tm + N_EXPERTS
    e_flat = expert_ids.reshape(-1)
    onehot = (e_flat[:, None] == jnp.arange(N_EXPERTS, dtype=jnp.int32)[None, :]).astype(jnp.int32)
    csum = jnp.cumsum(onehot, axis=0)
    rank = jnp.sum(csum * onehot, axis=1) - 1
    counts = csum[-1]
    padded = ((counts + tm - 1) // tm) * tm
    ends = jnp.cumsum(padded)
    offs = ends - padded
    pos = jnp.sum(onehot * offs[None, :], axis=1) + rank
    row_token = jnp.zeros((n_tiles * tm,), jnp.int32).at[pos].set(jnp.arange(n_assign, dtype=jnp.int32) // 2)
    tile_start = jnp.arange(n_tiles, dtype=jnp.int32) * tm
    tile_expert = jnp.minimum(jnp.sum((ends[None, :] <= tile_start[:, None]).astype(jnp.int32), axis=1),
                              N_EXPERTS - 1)
    pos_tiles = pos.reshape(n_tokens // tc, tc, 2).transpose(0, 2, 1).reshape(-1)
    n_used = (ends[-1:] // tm).astype(jnp.int32)
    return tile_expert, row_token, n_used, pos_tiles


def kernel(x, norm1_g, w_in, hgrn_lb, hgrn_norm_g, mlstm_conv_w, mlstm_i_bias, mlstm_f_bias, mlstm_norm_g,
           diff_qn_g, diff_kn_g, diff_lam_q1, diff_lam_k1, diff_lam_q2, diff_lam_k2, diff_subln_g, w_branch_a,
           w_branch_b, w_branch_c, w_out, norm2_g, router_group_w, router_group_b, router_expert_w,
           router_expert_b, expert_w1, expert_w3, expert_w2):
    bsz, s_len, d = x.shape
    depth = w_in.shape[0]
    t = bsz * s_len
    a_w, b_w, c_w = A_HEADS * HEAD_DIM, B_HEADS * HEAD_DIM, C_HEADS * HEAD_DIM
    o_b = 4 * a_w
    o_s = o_b + 4 * b_w
    o_c = o_s + 2 * B_HEADS
    o_g = o_c + 3 * c_w

    tables = _rope_tables(s_len)
    p_lb = jax.nn.softmax(hgrn_lb.astype(F32), axis=0)
    lb_all = jnp.cumsum(p_lb, axis=0) - p_lb[0:1]

    w_in_t = jnp.swapaxes(w_in, 1, 2)
    ff = expert_w1.shape[-1]
    ew1 = expert_w1.astype(BF16).reshape(depth * N_EXPERTS, d, ff)
    ew3 = expert_w3.astype(BF16).reshape(depth * N_EXPERTS, d, ff)
    ew2 = expert_w2.astype(BF16).reshape(depth * N_EXPERTS, ff, d)

    xf = x.reshape(t, d)
    xn, xn8 = _rmsnorm(xf, norm1_g[0])
    for l in range(depth):
        h_a = _matmul_wt(xn, w_in_t, l, BF16, 0, o_b, "in_proj_a").reshape(bsz, s_len, -1)
        h_b = _matmul_wt(xn, w_in_t, l, BF16, o_b, o_s - o_b, "in_proj_b").reshape(bsz, s_len, -1)
        h_c = _matmul_wt(xn, w_in_t, l, BF16, o_c, o_g - o_c, "in_proj_c").reshape(bsz, s_len, -1)
        gates = _matmul_wt(xn8, w_in_t, l, BF16, o_g, w_in.shape[2] - o_g, "in_proj_gates", quantize=True)
        h_s = _matmul_wt(xn, w_in_t, l, F32, o_s, LANES, "in_proj_scalar_gates").reshape(bsz, s_len, LANES)
        gate_i = h_s[..., :B_HEADS] + mlstm_i_bias[l]
        gate_f = h_s[..., B_HEADS:2 * B_HEADS] + mlstm_f_bias[l]

        out_a = _hgrn2(h_a, lb_all[l], hgrn_norm_g[l])
        out_b = _mlstm(h_b, mlstm_conv_w[l], gate_i, gate_f, mlstm_norm_g[l])
        lambda_init = 0.8 - 0.6 * math.exp(-0.3 * l)
        lam_vecs = jnp.stack([diff_lam_q1[l], diff_lam_k1[l], diff_lam_q2[l], diff_lam_k2[l]]).astype(F32)
        out_c = _diff_attn(h_c, tables, diff_qn_g[l], diff_kn_g[l], lam_vecs, diff_subln_g[l], lambda_init)

        y = _merge(out_a.reshape(t, a_w), out_b.reshape(t, b_w), out_c.reshape(t, c_w),
                   w_branch_a, w_branch_b, w_branch_c, l, gates)
        xf = _matmul(y, w_out, l, F32, residual=xf, name="out_proj")

        n_router = N_GROUPS + N_EXPERTS
        w_router = jnp.pad(jnp.concatenate([router_group_w[l], router_expert_w[l]], axis=1),
                           ((0, 0), (0, LANES - n_router)))
        b_router = jnp.pad(jnp.concatenate([router_group_b[l], router_expert_b[l]]),
                           (0, LANES - n_router)).reshape(1, LANES)
        xq, route = _router(xf, norm2_g[l], w_router, b_router)
        tile_expert, row_token, n_used, pos_tiles = _route_plan(route[:, 0:2].astype(jnp.int32), t)
        y_sorted = _experts(xq, ew1, ew3, ew2, l, tile_expert, row_token, n_used)
        xf, xn, xn8 = _combine(y_sorted, xf, route, pos_tiles, norm1_g[l + 1] if l + 1 < depth else None)
    return xf.reshape(bsz, s_len, d)
```

```python
import functools
import math

import jax
import jax.numpy as jnp
from jax import lax
from jax.experimental import pallas as pl
from jax.experimental.pallas import tpu as pltpu

F32 = jnp.float32
BF16 = jnp.bfloat16
FP8 = jnp.float8_e4m3fn
FP8_MAX = 448.0
FP8_TARGET = 224.0

HEAD_DIM = 128
A_HEADS = 8
B_HEADS = 12
C_HEADS = 12
C_DQK = 64
CONV_K = 4
ROPE_THETA = 500000.0
ROPE_DIM = C_DQK // 4
ROPE_HALF = ROPE_DIM // 2
N_GROUPS = 4
EXPERTS_PER_GROUP = 8
N_EXPERTS = N_GROUPS * EXPERTS_PER_GROUP
EPS = 1e-6
LANES = 128
NEG_BIG = -1e30

VMEM_LIMIT = 56 * 1024 * 1024

HGRN_CHUNK = 128
HGRN_SUB = 32
HGRN_EXP_CLAMP = 60.0
MLSTM_CHUNK = 128
ATT_BLOCK = 256
MASK_CHUNK = 64
MOE_TILE = 256
COMBINE_TILE = 128
GATHER_UNROLL = 8


def _cparams(*sem):
    return pltpu.CompilerParams(dimension_semantics=sem, vmem_limit_bytes=VMEM_LIMIT)


def _sigmoid(x):
    return 1.0 / (1.0 + jnp.exp(-x))


def _dot(a, b):
    return jnp.dot(a, b, preferred_element_type=F32)


def _dot_nt(a, b):
    return lax.dot_general(a, b, (((1,), (1,)), ((), ())), preferred_element_type=F32)


def _dot_tn(a, b):
    return lax.dot_general(a, b, (((0,), (0,)), ((), ())), preferred_element_type=F32)


def _split_dot(exact_lhs, x):
    hi = x.astype(BF16)
    lo = (x - hi.astype(F32)).astype(BF16)
    return _dot(exact_lhs, hi) + _dot(exact_lhs, lo)


def _split_dot_rhs(x, exact_rhs):
    hi = x.astype(BF16)
    lo = (x - hi.astype(F32)).astype(BF16)
    return _dot(hi, exact_rhs) + _dot(lo, exact_rhs)


def _to_fp8(y):
    return jnp.clip(y, -FP8_MAX, FP8_MAX).astype(FP8)


def _rmsnorm_kernel(x_ref, g_ref, o_ref, o8_ref):
    x = x_ref[...]
    y = x * lax.rsqrt(jnp.mean(x * x, axis=-1, keepdims=True) + EPS) * g_ref[...]
    o_ref[...] = y.astype(o_ref.dtype)
    o8_ref[...] = _to_fp8(y)


def _rmsnorm(x, g, tm=256):
    t, d = x.shape
    tile = pl.BlockSpec((tm, d), lambda i: (i, 0))
    return pl.pallas_call(
        _rmsnorm_kernel,
        grid=(t // tm,),
        in_specs=[tile, pl.BlockSpec((1, d), lambda i: (0, 0))],
        out_specs=[tile, tile],
        out_shape=[jax.ShapeDtypeStruct((t, d), BF16), jax.ShapeDtypeStruct((t, d), FP8)],
        compiler_params=_cparams("parallel"),
        name="rmsnorm",
    )(x, g.reshape(1, d))


def _mm_kernel(x_ref, w_ref, *rest, has_residual, cast_weight):
    r_ref = rest[0] if has_residual else None
    o_ref = rest[1] if has_residual else rest[0]
    if cast_weight:
        wb_ref = rest[-1]

        @pl.when(pl.program_id(1) == 0)
        def _():
            wb_ref[...] = w_ref[0].astype(BF16)

        acc = _dot(x_ref[...], wb_ref[...])
    else:
        acc = _dot(x_ref[...], w_ref[0])
    if has_residual:
        acc = acc + r_ref[...]
    o_ref[...] = acc.astype(o_ref.dtype)


def _pick_tile(n, candidates):
    for c in candidates:
        if n % c == 0:
            return c
    return n


def _matmul(x, w, layer, out_dtype, col0=0, ncols=None, residual=None, name="matmul"):
    m, k = x.shape
    n = w.shape[2] - col0 if ncols is None else ncols
    cast_weight = w.dtype != BF16
    tm = _pick_tile(m, (1024, 512, 256, 128))
    tn = _pick_tile(math.gcd(n, col0) if col0 else n, (512,) if cast_weight else (1024, 768, 512, 256, 128))
    cb = col0 // tn
    in_specs = [pl.BlockSpec((tm, k), lambda j, i: (i, 0)),
                pl.BlockSpec((1, k, tn), lambda j, i: (layer, 0, cb + j))]
    args = [x, w]
    if residual is not None:
        in_specs.append(pl.BlockSpec((tm, tn), lambda j, i: (i, j)))
        args.append(residual)
    return pl.pallas_call(
        functools.partial(_mm_kernel, has_residual=residual is not None, cast_weight=cast_weight),
        grid=(n // tn, m // tm),
        in_specs=in_specs,
        out_specs=pl.BlockSpec((tm, tn), lambda j, i: (i, j)),
        out_shape=jax.ShapeDtypeStruct((m, n), out_dtype),
        scratch_shapes=[pltpu.VMEM((k, tn), BF16)] if cast_weight else [],
        compiler_params=_cparams("parallel", "arbitrary"),
        name=name,
    )(*args)


WT_TILE = 512
WT_EDGE = 32


def _mm_wt_kernel(x_ref, a_ref, *rest, shift, quantize):
    b_ref = rest[0] if shift else None
    rest = rest[1:] if shift else rest
    o_ref, wb_ref = rest[0], rest[1]
    inv_ref = rest[2] if quantize else None

    @pl.when(pl.program_id(1) == 0)
    def _():
        if shift:
            tn = a_ref.shape[1]
            w = jnp.concatenate([a_ref[0, shift:tn, :], b_ref[0, 0:shift, :]], axis=0)
        else:
            w = a_ref[0]
        if quantize:
            top = jnp.max(jnp.max(jnp.abs(w), axis=1, keepdims=True), axis=0, keepdims=True)
            top = jnp.maximum(top, 1e-30)
            wb_ref[...] = (w * (FP8_TARGET / top)).astype(FP8)
            inv_ref[...] = top * (1.0 / FP8_TARGET)
        else:
            wb_ref[...] = w.astype(BF16)

    acc = _dot_nt(x_ref[...], wb_ref[...])
    if quantize:
        acc = acc * inv_ref[...]
    if len(o_ref.shape) == 3:
        for hh in range(o_ref.shape[0]):
            o_ref[hh] = acc[:, hh * LANES:(hh + 1) * LANES].astype(o_ref.dtype)
    else:
        o_ref[...] = acc.astype(o_ref.dtype)


def _matmul_wt(x, wt, layer, out_dtype, row0, nrows, name, quantize=False, head_major=False):
    m, k = x.shape
    tn = _pick_tile(nrows, (WT_TILE, LANES))
    shift = row0 % tn
    base = row0 - shift
    assert shift % 8 == 0 and shift <= WT_EDGE and tn % WT_EDGE == 0 and base % tn == 0
    tm = _pick_tile(m, (2048, 1024, 512, 256, 128) if quantize else (1024, 512, 256, 128))
    cb = base // tn
    in_specs = [pl.BlockSpec((tm, k), lambda j, i: (i, 0)),
                pl.BlockSpec((1, tn, k), lambda j, i: (layer, cb + j, 0))]
    args = [x, wt]
    if shift:
        per = tn // WT_EDGE
        in_specs.append(pl.BlockSpec((1, WT_EDGE, k), lambda j, i: (layer, (cb + j + 1) * per, 0)))
        args.append(wt)
    return pl.pallas_call(
        functools.partial(_mm_wt_kernel, shift=shift, quantize=quantize),
        grid=(nrows // tn, m // tm),
        in_specs=in_specs,
        out_specs=(pl.BlockSpec((tn // LANES, tm, LANES), lambda j, i: (j, i, 0)) if head_major
                   else pl.BlockSpec((tm, tn), lambda j, i: (i, j))),
        out_shape=jax.ShapeDtypeStruct((nrows // LANES, m, LANES) if head_major else (m, nrows), out_dtype),
        scratch_shapes=([pltpu.VMEM((tn, k), FP8), pltpu.VMEM((1, 1), F32)] if quantize
                        else [pltpu.VMEM((tn, k), BF16)]),
        compiler_params=_cparams("parallel", "arbitrary"),
        name=name,
    )(*args)


def _hgrn2_head(qp, z, v, gp, lb, ng, st):
    ch, sb = HGRN_CHUNK, HGRN_SUB
    nsb = ch // sb
    sig = _sigmoid(z)
    logf = jnp.log(lb + (1.0 - lb) * sig)
    k = (1.0 - lb) * (1.0 - sig)
    q = qp * _sigmoid(qp)

    row = lax.broadcasted_iota(jnp.int32, (ch, ch), 0)
    col = lax.broadcasted_iota(jnp.int32, (ch, ch), 1)
    causal = row >= col
    tri = jnp.where(causal, 1.0, 0.0).astype(BF16)
    b = _split_dot(tri, logf)
    b_last = b[ch - 1:ch, :]

    o = _dot_nt((q * jnp.exp(b)).astype(BF16), st.astype(BF16))

    refs = [b[i * sb:i * sb + 1, :] for i in range(nsb)]
    rblk = jnp.concatenate([jnp.broadcast_to(r, (sb, HEAD_DIM)) for r in refs], axis=0)
    qt = q * jnp.exp(b - rblk)
    rowblk = lax.broadcasted_iota(jnp.int32, (ch, HEAD_DIM), 0) // sb
    qcat = jnp.concatenate([jnp.where(rowblk == i, qt, 0.0).astype(BF16) for i in range(nsb)], axis=1)
    kcat = jnp.concatenate(
        [(k * jnp.exp(jnp.minimum(r - b, HGRN_EXP_CLAMP))).astype(BF16) for r in refs], axis=1)
    s = jnp.where(causal, _dot_nt(qcat, kcat), 0.0)
    o = o + _dot(s.astype(BF16), v)

    kd = (k * jnp.exp(b_last - b)).astype(BF16)
    st_new = st * jnp.exp(b_last) + _dot_tn(v, kd)

    y = o * lax.rsqrt(jnp.mean(o * o, axis=-1, keepdims=True) + EPS) * ng
    return y * (gp * _sigmoid(gp)), st_new


def _hgrn2_kernel(q_ref, f_ref, i_ref, g_ref, lb_ref, ng_ref, o_ref, st_ref):
    @pl.when(pl.program_id(1) == 0)
    def _():
        st_ref[...] = jnp.zeros_like(st_ref)

    heads = range(A_HEADS)
    state = [st_ref[h] for h in heads]
    new_state = []
    for h in heads:
        cols = slice(h * HEAD_DIM, (h + 1) * HEAD_DIM)
        y, st = _hgrn2_head(q_ref[0, :, cols].astype(F32), f_ref[0, :, cols].astype(F32), i_ref[0, :, cols],
                            g_ref[0, :, cols].astype(F32), lb_ref[:, cols], ng_ref[...], state[h])
        new_state.append(st)
        o_ref[0, :, cols] = y.astype(o_ref.dtype)
    for h in heads:
        st_ref[h] = new_state[h]


def _hgrn2(h_a, lb, norm_g):
    bsz, s_len, _ = h_a.shape
    ch = HGRN_CHUNK
    a_w = A_HEADS * HEAD_DIM

    def col(j):
        return pl.BlockSpec((1, ch, a_w), lambda b, c, j=j: (b, c, j))

    return pl.pallas_call(
        _hgrn2_kernel,
        grid=(bsz, s_len // ch),
        in_specs=[col(0), col(1), col(2), col(3),
                  pl.BlockSpec((1, a_w), lambda b, c: (0, 0)),
                  pl.BlockSpec((1, HEAD_DIM), lambda b, c: (0, 0))],
        out_specs=pl.BlockSpec((1, ch, a_w), lambda b, c: (b, c, 0)),
        out_shape=jax.ShapeDtypeStruct((bsz, s_len, a_w), BF16),
        scratch_shapes=[pltpu.VMEM((A_HEADS, HEAD_DIM, HEAD_DIM), F32)],
        compiler_params=_cparams("parallel", "arbitrary"),
        name="hgrn2",
    )(h_a, h_a, h_a, h_a, lb.reshape(1, -1), norm_g.reshape(1, HEAD_DIM))


def _log_sigmoid(x):
    return jnp.minimum(x, 0.0) - jnp.log(1.0 + jnp.exp(-jnp.abs(x)))


def _causal_conv_silu(x, prev, w):
    ch = x.shape[0]
    tail = prev[ch - 8:ch, :]
    rowi = lax.broadcasted_iota(jnp.int32, tail.shape, 0)
    acc = x * w[CONV_K - 1:CONV_K, :]
    for sft in range(1, CONV_K):
        xs = pltpu.roll(x, sft, 0)
        top = jnp.where(rowi >= sft, xs[0:8, :], pltpu.roll(tail, sft, 0))
        shifted = jnp.concatenate([top, xs[8:ch, :]], axis=0)
        acc = acc + shifted * w[CONV_K - 1 - sft:CONV_K - sft, :]
    return acc * _sigmoid(acc)


def _mlstm_kernel(q_ref, k_ref, v_ref, og_ref, w_ref, gi_ref, gf_ref, ng_ref,
                  o_ref, c_ref, n_ref, m_ref, qprev_ref, kprev_ref):
    @pl.when(pl.program_id(1) == 0)
    def _():
        c_ref[...] = jnp.zeros_like(c_ref)
        n_ref[...] = jnp.zeros_like(n_ref)
        m_ref[...] = jnp.zeros_like(m_ref)
        qprev_ref[...] = jnp.zeros_like(qprev_ref)
        kprev_ref[...] = jnp.zeros_like(kprev_ref)

    ch = MLSTM_CHUNK
    b_w = B_HEADS * HEAD_DIM
    heads = range(B_HEADS)
    row = lax.broadcasted_iota(jnp.int32, (ch, ch), 0)
    col = lax.broadcasted_iota(jnp.int32, (ch, ch), 1)
    tril = row >= col
    tril_bf = jnp.where(tril, 1.0, 0.0).astype(BF16)
    triu_bf = jnp.where(row <= col, 1.0, 0.0).astype(BF16)
    eye_bf = jnp.where(row == col, 1.0, 0.0).astype(BF16)
    ones_bf = jnp.ones((ch, HEAD_DIM), BF16)

    def cols(h):
        return slice(h * HEAD_DIM, (h + 1) * HEAD_DIM)

    def nt_split(exact_lhs, x):
        hi = x.astype(BF16)
        lo = (x - hi.astype(F32)).astype(BF16)
        return _dot_nt(exact_lhs, hi) + _dot_nt(exact_lhs, lo)

    qprev = qprev_ref[...]
    kprev = kprev_ref[...]
    qc, kc, g, inter, logw_c, logw_r, m_prev, f_last = [], [], [], [], [], [], [], []
    for h in heads:
        qc.append(_causal_conv_silu(q_ref[0, :, cols(h)].astype(F32), qprev[:, cols(h)], w_ref[:, cols(h)])
                  * (HEAD_DIM ** -0.5))
        kc.append(_causal_conv_silu(k_ref[0, :, cols(h)].astype(F32), kprev[:, cols(h)],
                                    w_ref[:, b_w + h * HEAD_DIM:b_w + (h + 1) * HEAD_DIM]))
        li_r = gi_ref[0, 0, h:h + 1, :]
        lf_r = _log_sigmoid(gf_ref[0, 0, h:h + 1, :])
        fc_c = nt_split(tril_bf, jnp.broadcast_to(lf_r, (ch, ch)))
        li_c = nt_split(eye_bf, jnp.broadcast_to(li_r, (ch, ch)))
        fc_r = _split_dot_rhs(jnp.broadcast_to(lf_r, (8, ch)), triu_bf)[0:1]
        g.append(jnp.where(tril, fc_c - fc_r + li_r, NEG_BIG))
        m_prev.append(m_ref[h])
        inter.append(fc_c + m_prev[h])
        f_last.append(fc_r[:, ch - 1:ch])
        logw_c.append(f_last[h] - fc_c + li_c)
        logw_r.append(f_last[h] - fc_r + li_r)
    qprev_ref[...] = q_ref[0].astype(F32)
    kprev_ref[...] = k_ref[0].astype(F32)

    m, num, den, m_new = [], [], [], []
    for h in heads:
        m.append(jnp.maximum(jnp.max(g[h], axis=1, keepdims=True), inter[h]))
        m_new.append(jnp.maximum(f_last[h] + m_prev[h], jnp.max(logw_r[h], axis=1, keepdims=True)))
    for h in heads:
        dmat = jnp.exp(g[h] - m[h])
        a = jnp.exp(inter[h] - m[h])
        qb = qc[h].astype(BF16)
        sc = (_dot_nt(qb, kc[h].astype(BF16)) * dmat).astype(BF16)
        n_b = jnp.broadcast_to(n_ref[h], (ch, HEAD_DIM)).astype(BF16)
        num.append(_dot(sc, v_ref[0, :, cols(h)]) + a * _dot(qb, c_ref[h].astype(BF16)))
        den.append(_dot(sc, ones_bf) + a * _dot_nt(qb, n_b))

    for h in heads:
        dec = jnp.exp(f_last[h] + m_prev[h] - m_new[h])
        kw = kc[h] * jnp.exp(logw_c[h] - m_new[h])
        c_ref[h] = dec * c_ref[h] + _dot_tn(kw.astype(BF16), v_ref[0, :, cols(h)])
        n_ref[h] = dec * n_ref[h] + jnp.sum(kw, axis=0, keepdims=True)
        m_ref[h] = m_new[h]

    hs = [num[h] / jnp.maximum(jnp.abs(den[h]), jnp.exp(-m[h])) for h in heads]
    ms = [_dot((hs[h] * hs[h]).astype(BF16), ones_bf) * (1.0 / HEAD_DIM) for h in heads]
    for h in heads:
        y = hs[h] * lax.rsqrt(ms[h] + EPS) * ng_ref[...]
        o_ref[0, :, cols(h)] = (y * _sigmoid(og_ref[0, :, cols(h)].astype(F32))).astype(o_ref.dtype)


def _mlstm(h_b, conv_w, gate_i, gate_f, norm_g):
    bsz, s_len, _ = h_b.shape
    ch = MLSTM_CHUNK
    assert ch == LANES == HEAD_DIM
    nc = s_len // ch
    b_w = B_HEADS * HEAD_DIM

    def col(j):
        return pl.BlockSpec((1, ch, b_w), lambda b, c, j=j: (b, c, j))

    def rows(gt):
        return gt.reshape(bsz, nc, ch, B_HEADS).transpose(0, 1, 3, 2)

    row_spec = pl.BlockSpec((1, 1, B_HEADS, ch), lambda b, c: (b, c, 0, 0))
    return pl.pallas_call(
        _mlstm_kernel,
        grid=(bsz, nc),
        in_specs=[col(0), col(1), col(2), col(3),
                  pl.BlockSpec((CONV_K, 2 * b_w), lambda b, c: (0, 0)),
                  row_spec, row_spec,
                  pl.BlockSpec((1, HEAD_DIM), lambda b, c: (0, 0))],
        out_specs=pl.BlockSpec((1, ch, b_w), lambda b, c: (b, c, 0)),
        out_shape=jax.ShapeDtypeStruct((bsz, s_len, b_w), BF16),
        scratch_shapes=[pltpu.VMEM((B_HEADS, HEAD_DIM, HEAD_DIM), F32), pltpu.VMEM((B_HEADS, 1, HEAD_DIM), F32),
                        pltpu.VMEM((B_HEADS, 1, 1), F32), pltpu.VMEM((ch, b_w), F32),
                        pltpu.VMEM((ch, b_w), F32)],
        compiler_params=_cparams("parallel", "arbitrary"),
        name="mlstm",
    )(h_b, h_b, h_b, h_b, conv_w, rows(gate_i), rows(gate_f), norm_g.reshape(1, HEAD_DIM))


def _diff_attn_kernel(q_ref, k_ref, v_ref, cos_ref, sa_ref, sb_ref, qg_ref, kg_ref, lam_ref, sg_ref,
                      o_ref, qs_ref, kp_ref, vx_ref, *, lambda_init):
    s_len = q_ref.shape[1]
    tq = ATT_BLOCK
    nq = s_len // tq
    lane = lax.broadcasted_iota(jnp.int32, (tq, LANES), 1)
    first_map = lane < C_DQK
    li = lax.broadcasted_iota(jnp.int32, (LANES, LANES), 0) // C_DQK
    lj = lax.broadcasted_iota(jnp.int32, (LANES, LANES), 1) // C_DQK
    same_map = jnp.where(li == lj, 1.0, 0.0).astype(BF16)

    def qk_norm_rope(x, g, rows):
        ss = _split_dot_rhs(x * x, same_map)
        y = x * lax.rsqrt(ss * (1.0 / C_DQK) + EPS) * g
        return (y * cos_ref[rows, :] + pltpu.roll(y, LANES - ROPE_HALF, 1) * sa_ref[rows, :]
                + pltpu.roll(y, ROPE_HALF, 1) * sb_ref[rows, :])

    def prep(i, carry):
        rows = pl.ds(pl.multiple_of(i * tq, tq), tq)
        qn = qk_norm_rope(q_ref[0, rows, :].astype(F32), qg_ref[...], rows) * (C_DQK ** -0.5)
        qs_ref[i, 0:tq, :] = jnp.where(first_map, qn, 0.0).astype(BF16)
        qs_ref[i, tq:2 * tq, :] = jnp.where(first_map, 0.0, qn).astype(BF16)
        kp_ref[rows, :] = qk_norm_rope(k_ref[0, rows, :].astype(F32), kg_ref[...], rows).astype(BF16)
        vx_ref[rows, 0:HEAD_DIM] = v_ref[0, rows, :]
        vx_ref[rows, HEAD_DIM:2 * HEAD_DIM] = jnp.ones((tq, HEAD_DIM), BF16)
        return carry

    lax.fori_loop(0, nq, prep, 0)

    lamv = lam_ref[...]
    lam = (jnp.exp(jnp.sum(lamv[0:1] * lamv[1:2], axis=1, keepdims=True))
           - jnp.exp(jnp.sum(lamv[2:3] * lamv[3:4], axis=1, keepdims=True)) + lambda_init)
    qchunk = (lax.broadcasted_iota(jnp.int32, (2 * tq, tq), 0) % tq) // MASK_CHUNK
    kchunk = lax.broadcasted_iota(jnp.int32, (2 * tq, tq), 1) // MASK_CHUNK
    diag_ok = qchunk >= kchunk

    for i in range(nq):
        k0 = i * tq
        qs = qs_ref[i]
        s_d = jnp.where(diag_ok, _dot_nt(qs, kp_ref[k0:k0 + tq, :]), NEG_BIG)
        m = jnp.max(s_d, axis=1, keepdims=True)
        if i > 0:
            s_o = _dot_nt(qs, kp_ref[0:k0, :])
            m = jnp.maximum(m, jnp.max(s_o, axis=1, keepdims=True))
            acc = _dot(jnp.exp(s_o - m).astype(BF16), vx_ref[0:k0, :])
            acc = acc + _dot(jnp.exp(s_d - m).astype(BF16), vx_ref[k0:k0 + tq, :])
        else:
            acc = _dot(jnp.exp(s_d - m).astype(BF16), vx_ref[k0:k0 + tq, :])
        o = acc[:, 0:HEAD_DIM] / acc[:, HEAD_DIM:2 * HEAD_DIM]
        out = o[0:tq] - lam * o[tq:2 * tq]
        y = out * lax.rsqrt(jnp.mean(out * out, axis=-1, keepdims=True) + EPS) * sg_ref[...]
        o_ref[0, k0:k0 + tq, :] = (y * (1.0 - lambda_init)).astype(o_ref.dtype)


def _rope_tables(s_len):
    pos = jnp.arange(s_len, dtype=F32)
    inv_freq = ROPE_THETA ** (-jnp.arange(0, ROPE_DIM, 2, dtype=F32) / ROPE_DIM)
    ang = pos[:, None] * inv_freq[None, :]
    cos, sin = jnp.cos(ang), jnp.sin(ang)
    ones = jnp.ones((s_len, C_DQK - ROPE_DIM), F32)
    zeros = jnp.zeros((s_len, C_DQK - ROPE_DIM), F32)
    zh = jnp.zeros((s_len, ROPE_HALF), F32)
    cos_map = jnp.concatenate([cos, cos, ones], axis=1)
    sa_map = jnp.concatenate([-sin, zh, zeros], axis=1)
    sb_map = jnp.concatenate([zh, sin, zeros], axis=1)
    return tuple(jnp.concatenate([t, t], axis=1) for t in (cos_map, sa_map, sb_map))


def _diff_attn(h_c, bsz, tables, qn_g, kn_g, lam_vecs, subln_g, lambda_init):
    s_len = h_c.shape[1]

    def col(j):
        return pl.BlockSpec((1, s_len, HEAD_DIM), lambda b, h, j=j: ((j * C_HEADS + h) * bsz + b, 0, 0))

    def full(shape):
        return pl.BlockSpec(shape, lambda b, h: (0,) * len(shape))

    return pl.pallas_call(
        functools.partial(_diff_attn_kernel, lambda_init=lambda_init),
        grid=(bsz, C_HEADS),
        in_specs=[col(0), col(1), col(2), full((s_len, LANES)), full((s_len, LANES)), full((s_len, LANES)),
                  full((1, LANES)), full((1, LANES)), full((4, C_DQK)), full((1, HEAD_DIM))],
        out_specs=pl.BlockSpec((1, s_len, HEAD_DIM), lambda b, h: (b, 0, h)),
        out_shape=jax.ShapeDtypeStruct((bsz, s_len, C_HEADS * HEAD_DIM), BF16),
        scratch_shapes=[pltpu.VMEM((s_len // ATT_BLOCK, 2 * ATT_BLOCK, LANES), BF16),
                        pltpu.VMEM((s_len, LANES), BF16), pltpu.VMEM((s_len, 2 * HEAD_DIM), BF16)],
        compiler_params=_cparams("parallel", "parallel"),
        name="diff_attn",
    )(h_c, h_c, h_c, *tables, jnp.tile(qn_g, 2).reshape(1, LANES), jnp.tile(kn_g, 2).reshape(1, LANES),
      lam_vecs, subln_g.reshape(1, HEAD_DIM))


def _merge_kernel(oa_ref, ob_ref, oc_ref, wa_ref, wb_ref, wc_ref, ga_ref, gb_ref, gc_ref, y_ref,
                  wab_ref, wbb_ref, wcb_ref):
    @pl.when(pl.program_id(1) == 0)
    def _():
        wab_ref[...] = wa_ref[0].astype(BF16)
        wbb_ref[...] = wb_ref[0].astype(BF16)
        wcb_ref[...] = wc_ref[0].astype(BF16)

    y = _sigmoid(ga_ref[...].astype(F32)) * _dot(oa_ref[...], wab_ref[...])
    y = y + _sigmoid(gb_ref[...].astype(F32)) * _dot(ob_ref[...], wbb_ref[...])
    y = y + _sigmoid(gc_ref[...].astype(F32)) * _dot(oc_ref[...], wcb_ref[...])
    y_ref[...] = y.astype(y_ref.dtype)


def _merge(oa, ob, oc, wa, wb, wc, layer, gates):
    m = oa.shape[0]
    n = wa.shape[2]
    tm = _pick_tile(m, (1024, 512, 256, 128))
    tn = _pick_tile(n, (512, 256, 128))
    nb = n // tn

    def lhs(a):
        return pl.BlockSpec((tm, a.shape[1]), lambda j, i: (i, 0))

    def rhs(w):
        return pl.BlockSpec((1, w.shape[1], tn), lambda j, i: (layer, 0, j))

    def gate(k):
        return pl.BlockSpec((tm, tn), lambda j, i, k=k: (i, k * nb + j))

    return pl.pallas_call(
        _merge_kernel,
        grid=(nb, m // tm),
        in_specs=[lhs(oa), lhs(ob), lhs(oc), rhs(wa), rhs(wb), rhs(wc), gate(0), gate(1), gate(2)],
        out_specs=pl.BlockSpec((tm, tn), lambda j, i: (i, j)),
        out_shape=jax.ShapeDtypeStruct((m, n), BF16),
        scratch_shapes=[pltpu.VMEM((w.shape[1], tn), BF16) for w in (wa, wb, wc)],
        compiler_params=_cparams("parallel", "arbitrary"),
        name="merge",
    )(oa, ob, oc, wa, wb, wc, gates, gates, gates)


def _pack_bf16_pairs(x):
    n = x.shape[1] // 2
    lo = lax.bitcast_convert_type(x[:, :n].astype(BF16).astype(F32), jnp.uint32)
    hi = lax.bitcast_convert_type(x[:, n:].astype(BF16).astype(F32), jnp.uint32)
    return (hi & jnp.uint32(0xFFFF0000)) | (lo >> 16)


def _unpack_bf16_pairs(u):
    lo = lax.bitcast_convert_type(u << 16, F32)
    hi = lax.bitcast_convert_type(u & jnp.uint32(0xFFFF0000), F32)
    return lo, hi


def _router_kernel(x_ref, g_ref, w_ref, b_ref, xq_ref, route_ref):
    x = x_ref[...]
    xn = x * lax.rsqrt(jnp.mean(x * x, axis=-1, keepdims=True) + EPS) * g_ref[...]
    xq_ref[...] = _pack_bf16_pairs(xn)
    logits = jnp.dot(xn, w_ref[...], preferred_element_type=F32, precision=lax.Precision.HIGHEST) + b_ref[...]
    lane = lax.broadcasted_iota(jnp.int32, logits.shape, 1).astype(F32)
    none = float(LANES)

    def first_argmax(vals):
        top = jnp.max(vals, axis=1, keepdims=True)
        return top, jnp.min(jnp.where(vals == top, lane, none), axis=1, keepdims=True)

    is_group = lane < N_GROUPS
    g_top, g_idx = first_argmax(jnp.where(is_group, logits, NEG_BIG))
    g_w = 1.0 / jnp.sum(jnp.where(is_group, jnp.exp(logits - g_top), 0.0), axis=1, keepdims=True)
    lo = N_GROUPS + EXPERTS_PER_GROUP * g_idx
    e_logits = jnp.where((lane >= lo) & (lane < lo + EXPERTS_PER_GROUP), logits, NEG_BIG)
    v1, i1 = first_argmax(e_logits)
    v2, i2 = first_argmax(jnp.where(lane == i1, NEG_BIG, e_logits))
    e2 = jnp.exp(v2 - v1)
    w1 = g_w / (1.0 + e2)
    w2 = w1 * e2
    route = jnp.where(lane == 0, i1 - N_GROUPS, jnp.where(lane == 1, i2 - N_GROUPS,
                      jnp.where(lane == 2, w1, jnp.where(lane == 3, w2, 0.0))))
    route_ref[...] = route


def _router(x, g, w_router, b_router, tm=256):
    t, d = x.shape
    return pl.pallas_call(
        _router_kernel,
        grid=(t // tm,),
        in_specs=[pl.BlockSpec((tm, d), lambda i: (i, 0)), pl.BlockSpec((1, d), lambda i: (0, 0)),
                  pl.BlockSpec((d, LANES), lambda i: (0, 0)), pl.BlockSpec((1, LANES), lambda i: (0, 0))],
        out_specs=[pl.BlockSpec((tm, d // 2), lambda i: (i, 0)), pl.BlockSpec((tm, LANES), lambda i: (i, 0))],
        out_shape=[jax.ShapeDtypeStruct((t, d // 2), jnp.uint32), jax.ShapeDtypeStruct((t, LANES), F32)],
        compiler_params=_cparams("parallel"),
        name="router",
    )(x, g.reshape(1, d), w_router, b_router)


def _gather_rows(src_hbm, dst_ref, idx_ref, base, n_rows, sem):
    def body(j, carry):
        for u in range(GATHER_UNROLL):
            r = j * GATHER_UNROLL + u
            pltpu.make_async_copy(src_hbm.at[pl.ds(idx_ref[base + r], 1)], dst_ref.at[pl.ds(r, 1)],
                                  sem).start(priority=u % 2)
        return carry
    lax.fori_loop(0, n_rows // GATHER_UNROLL, body, 0)


def _wait_rows(src_hbm, dst_ref, n_rows, sem):
    pltpu.make_async_copy(src_hbm.at[pl.ds(0, n_rows)], dst_ref, sem).wait()


def _expert_kernel(tile_expert_ref, row_token_ref, n_used_ref, x_hbm, w1_ref, w3_ref, w2_ref, y_ref, xbuf, sem):
    del tile_expert_ref
    i = pl.program_id(0)
    n_used = n_used_ref[0]
    tm = MOE_TILE
    half = w1_ref.shape[1] // 2
    slot = i % 2

    @pl.when((i == 0) & (n_used > 0))
    def _():
        _gather_rows(x_hbm, xbuf.at[0], row_token_ref, 0, tm, sem.at[0])

    def prefetch_rows(part):
        base = (i + 1) * tm
        for r in range(part * (tm // 4), (part + 1) * (tm // 4)):
            pltpu.make_async_copy(x_hbm.at[pl.ds(row_token_ref[base + r], 1)], xbuf.at[1 - slot, pl.ds(r, 1)],
                                  sem.at[1 - slot]).start(priority=r % 2)

    def ffn(prefetch):
        _wait_rows(x_hbm, xbuf.at[slot], tm, sem.at[slot])
        lo, hi = _unpack_bf16_pairs(xbuf[slot])
        lo = lo.astype(BF16)
        hi = hi.astype(BF16)
        if prefetch:
            prefetch_rows(0)
        h1 = _dot(lo, w1_ref[0, 0:half, :]) + _dot(hi, w1_ref[0, half:2 * half, :])
        if prefetch:
            prefetch_rows(1)
        h3 = _dot(lo, w3_ref[0, 0:half, :]) + _dot(hi, w3_ref[0, half:2 * half, :])
        if prefetch:
            prefetch_rows(2)
        h = (h1 * _sigmoid(h1) * h3).astype(BF16)
        y = _dot(h, w2_ref[0])
        if prefetch:
            prefetch_rows(3)
        y_ref[...] = _pack_bf16_pairs(y)

    @pl.when(i + 1 < n_used)
    def _():
        ffn(True)

    @pl.when(i + 1 == n_used)
    def _():
        ffn(False)

    @pl.when(i >= n_used)
    def _():
        y_ref[...] = jnp.zeros_like(y_ref)


def _experts(xq, w1, w3, w2, layer, tile_expert, row_token, n_used):
    t, dh = xq.shape
    d = 2 * dh
    ff = w1.shape[2]
    tm = MOE_TILE
    n_tiles = tile_expert.shape[0]
    e0 = layer * N_EXPERTS
    grid_spec = pltpu.PrefetchScalarGridSpec(
        num_scalar_prefetch=3,
        grid=(n_tiles,),
        in_specs=[pl.BlockSpec(memory_space=pl.ANY),
                  pl.BlockSpec((1, d, ff), lambda i, te, rt, nu: (e0 + te[i], 0, 0)),
                  pl.BlockSpec((1, d, ff), lambda i, te, rt, nu: (e0 + te[i], 0, 0)),
                  pl.BlockSpec((1, ff, d), lambda i, te, rt, nu: (e0 + te[i], 0, 0))],
        out_specs=pl.BlockSpec((tm, dh), lambda i, te, rt, nu: (i, 0)),
        scratch_shapes=[pltpu.VMEM((2, tm, dh), jnp.uint32), pltpu.SemaphoreType.DMA((2,))],
    )
    return pl.pallas_call(
        _expert_kernel,
        grid_spec=grid_spec,
        out_shape=jax.ShapeDtypeStruct((n_tiles * tm, dh), jnp.uint32),
        compiler_params=_cparams("arbitrary"),
        name="experts",
    )(tile_expert, row_token, n_used, xq, w1, w3, w2)


def _combine_kernel(pos_ref, y_hbm, x_ref, route_ref, *rest):
    if len(rest) == 6:
        g_ref, xo_ref, xn_ref, x8_ref, ybuf, sem = rest
    else:
        g_ref, xn_ref, x8_ref = None, None, None
        xo_ref, ybuf, sem = rest
    i = pl.program_id(0)
    n = pl.num_programs(0)
    tc = COMBINE_TILE
    slot = i % 2

    @pl.when(i == 0)
    def _():
        _gather_rows(y_hbm, ybuf.at[0], pos_ref, 0, 2 * tc, sem.at[0])

    @pl.when(i + 1 < n)
    def _():
        _gather_rows(y_hbm, ybuf.at[1 - slot], pos_ref, (i + 1) * 2 * tc, 2 * tc, sem.at[1 - slot])

    _wait_rows(y_hbm, ybuf.at[slot], 2 * tc, sem.at[slot])
    lo0, hi0 = _unpack_bf16_pairs(ybuf[slot, 0:tc, :])
    lo1, hi1 = _unpack_bf16_pairs(ybuf[slot, tc:2 * tc, :])
    w0 = route_ref[:, 2:3]
    w1 = route_ref[:, 3:4]
    x = x_ref[...] + jnp.concatenate([w0 * lo0 + w1 * lo1, w0 * hi0 + w1 * hi1], axis=1)
    xo_ref[...] = x
    if xn_ref is not None:
        y = x * lax.rsqrt(jnp.mean(x * x, axis=-1, keepdims=True) + EPS) * g_ref[...]
        xn_ref[...] = y.astype(xn_ref.dtype)
        x8_ref[...] = _to_fp8(y)


def _combine(y_sorted, x, route, pos_tiles, next_g=None):
    t, d = x.shape
    tc = COMBINE_TILE
    tile = pl.BlockSpec((tc, d), lambda i, p: (i, 0))
    in_specs = [pl.BlockSpec(memory_space=pl.ANY), tile, pl.BlockSpec((tc, LANES), lambda i, p: (i, 0))]
    out_specs = [tile]
    out_shape = [jax.ShapeDtypeStruct((t, d), F32)]
    args = [pos_tiles, y_sorted, x, route]
    if next_g is not None:
        in_specs.append(pl.BlockSpec((1, d), lambda i, p: (0, 0)))
        out_specs += [tile, tile]
        out_shape += [jax.ShapeDtypeStruct((t, d), BF16), jax.ShapeDtypeStruct((t, d), FP8)]
        args.append(next_g.reshape(1, d))
    grid_spec = pltpu.PrefetchScalarGridSpec(
        num_scalar_prefetch=1,
        grid=(t // tc,),
        in_specs=in_specs,
        out_specs=out_specs,
        scratch_shapes=[pltpu.VMEM((2, 2 * tc, d // 2), jnp.uint32), pltpu.SemaphoreType.DMA((2,))],
    )
    out = pl.pallas_call(
        _combine_kernel,
        grid_spec=grid_spec,
        out_shape=out_shape,
        compiler_params=_cparams("arbitrary"),
        name="combine",
    )(*args)
    return tuple(out) if next_g is not None else (out[0], None, None)


def _route_plan(expert_ids, n_tokens):
    tm, tc = MOE_TILE, COMBINE_TILE
    n_assign = 2 * n_tokens
    n_tiles = n_assign // tm + N_EXPERTS
    e_flat = expert_ids.reshape(-1)
    onehot = (e_flat[:, None] == jnp.arange(N_EXPERTS, dtype=jnp.int32)[None, :]).astype(jnp.int32)
    csum = jnp.cumsum(onehot, axis=0)
    rank = jnp.sum(csum * onehot, axis=1) - 1
    counts = csum[-1]
    padded = ((counts + tm - 1) // tm) * tm
    ends = jnp.cumsum(padded)
    offs = ends - padded
    pos = jnp.sum(onehot * offs[None, :], axis=1) + rank
    row_token = jnp.zeros((n_tiles * tm,), jnp.int32).at[pos].set(jnp.arange(n_assign, dtype=jnp.int32) // 2)
    tile_start = jnp.arange(n_tiles, dtype=jnp.int32) * tm
    tile_expert = jnp.minimum(jnp.sum((ends[None, :] <= tile_start[:, None]).astype(jnp.int32), axis=1),
                              N_EXPERTS - 1)
    pos_tiles = pos.reshape(n_tokens // tc, tc, 2).transpose(0, 2, 1).reshape(-1)
    n_used = (ends[-1:] // tm).astype(jnp.int32)
    return tile_expert, row_token, n_used, pos_tiles


def kernel(x, norm1_g, w_in, hgrn_lb, hgrn_norm_g, mlstm_conv_w, mlstm_i_bias, mlstm_f_bias, mlstm_norm_g,
           diff_qn_g, diff_kn_g, diff_lam_q1, diff_lam_k1, diff_lam_q2, diff_lam_k2, diff_subln_g, w_branch_a,
           w_branch_b, w_branch_c, w_out, norm2_g, router_group_w, router_group_b, router_expert_w,
           router_expert_b, expert_w1, expert_w3, expert_w2):
    bsz, s_len, d = x.shape
    depth = w_in.shape[0]
    t = bsz * s_len
    a_w, b_w, c_w = A_HEADS * HEAD_DIM, B_HEADS * HEAD_DIM, C_HEADS * HEAD_DIM
    o_b = 4 * a_w
    o_s = o_b + 4 * b_w
    o_c = o_s + 2 * B_HEADS
    o_g = o_c + 3 * c_w

    tables = _rope_tables(s_len)
    p_lb = jax.nn.softmax(hgrn_lb.astype(F32), axis=0)
    lb_all = jnp.cumsum(p_lb, axis=0) - p_lb[0:1]

    w_in_t = jnp.swapaxes(w_in, 1, 2)
    ff = expert_w1.shape[-1]
    ew1 = expert_w1.astype(BF16).reshape(depth * N_EXPERTS, d, ff)
    ew3 = expert_w3.astype(BF16).reshape(depth * N_EXPERTS, d, ff)
    ew2 = expert_w2.astype(BF16).reshape(depth * N_EXPERTS, ff, d)

    xf = x.reshape(t, d)
    xn, xn8 = _rmsnorm(xf, norm1_g[0])
    for l in range(depth):
        h_a = _matmul_wt(xn, w_in_t, l, BF16, 0, o_b, "in_proj_a").reshape(bsz, s_len, -1)
        h_b = _matmul_wt(xn, w_in_t, l, BF16, o_b, o_s - o_b, "in_proj_b").reshape(bsz, s_len, -1)
        h_c = _matmul_wt(xn, w_in_t, l, BF16, o_c, o_g - o_c, "in_proj_c", head_major=True)
        h_c = h_c.reshape(-1, s_len, HEAD_DIM)
        gates = _matmul_wt(xn8, w_in_t, l, BF16, o_g, w_in.shape[2] - o_g, "in_proj_gates", quantize=True)
        h_s = _matmul_wt(xn, w_in_t, l, F32, o_s, LANES, "in_proj_scalar_gates").reshape(bsz, s_len, LANES)
        gate_i = h_s[..., :B_HEADS] + mlstm_i_bias[l]
        gate_f = h_s[..., B_HEADS:2 * B_HEADS] + mlstm_f_bias[l]

        out_a = _hgrn2(h_a, lb_all[l], hgrn_norm_g[l])
        out_b = _mlstm(h_b, mlstm_conv_w[l], gate_i, gate_f, mlstm_norm_g[l])
        lambda_init = 0.8 - 0.6 * math.exp(-0.3 * l)
        lam_vecs = jnp.stack([diff_lam_q1[l], diff_lam_k1[l], diff_lam_q2[l], diff_lam_k2[l]]).astype(F32)
        out_c = _diff_attn(h_c, bsz, tables, diff_qn_g[l], diff_kn_g[l], lam_vecs, diff_subln_g[l], lambda_init)

        y = _merge(out_a.reshape(t, a_w), out_b.reshape(t, b_w), out_c.reshape(t, c_w),
                   w_branch_a, w_branch_b, w_branch_c, l, gates)
        xf = _matmul(y, w_out, l, F32, residual=xf, name="out_proj")

        n_router = N_GROUPS + N_EXPERTS
        w_router = jnp.pad(jnp.concatenate([router_group_w[l], router_expert_w[l]], axis=1),
                           ((0, 0), (0, LANES - n_router)))
        b_router = jnp.pad(jnp.concatenate([router_group_b[l], router_expert_b[l]]),
                           (0, LANES - n_router)).reshape(1, LANES)
        xq, route = _router(xf, norm2_g[l], w_router, b_router)
        tile_expert, row_token, n_used, pos_tiles = _route_plan(route[:, 0:2].astype(jnp.int32), t)
        y_sorted = _experts(xq, ew1, ew3, ew2, l, tile_expert, row_token, n_used)
        xf, xn, xn8 = _combine(y_sorted, xf, route, pos_tiles, norm1_g[l + 1] if l + 1 < depth else None)
    return xf.reshape(bsz, s_len, d)
```

```python
import functools
import math

import jax
import jax.numpy as jnp
from jax import lax
from jax.experimental import pallas as pl
from jax.experimental.pallas import tpu as pltpu

F32 = jnp.float32
BF16 = jnp.bfloat16
FP8 = jnp.float8_e4m3fn
FP8_MAX = 448.0
FP8_TARGET = 224.0

HEAD_DIM = 128
A_HEADS = 8
B_HEADS = 12
C_HEADS = 12
C_DQK = 64
CONV_K = 4
ROPE_THETA = 500000.0
ROPE_DIM = C_DQK // 4
ROPE_HALF = ROPE_DIM // 2
N_GROUPS = 4
EXPERTS_PER_GROUP = 8
N_EXPERTS = N_GROUPS * EXPERTS_PER_GROUP
EPS = 1e-6
LANES = 128
NEG_BIG = -1e30

VMEM_LIMIT = 56 * 1024 * 1024

HGRN_CHUNK = 128
HGRN_SUB = 32
HGRN_EXP_CLAMP = 60.0
MLSTM_CHUNK = 128
ATT_BLOCK = 256
MASK_CHUNK = 64
MOE_TILE = 256
COMBINE_TILE = 128
GATHER_UNROLL = 8


def _cparams(*sem):
    return pltpu.CompilerParams(dimension_semantics=sem, vmem_limit_bytes=VMEM_LIMIT)


def _sigmoid(x):
    return 1.0 / (1.0 + jnp.exp(-x))


def _dot(a, b):
    return jnp.dot(a, b, preferred_element_type=F32)


def _dot_nt(a, b):
    return lax.dot_general(a, b, (((1,), (1,)), ((), ())), preferred_element_type=F32)


def _dot_tn(a, b):
    return lax.dot_general(a, b, (((0,), (0,)), ((), ())), preferred_element_type=F32)


def _split_dot(exact_lhs, x):
    hi = x.astype(BF16)
    lo = (x - hi.astype(F32)).astype(BF16)
    return _dot(exact_lhs, hi) + _dot(exact_lhs, lo)


def _split_dot_rhs(x, exact_rhs):
    hi = x.astype(BF16)
    lo = (x - hi.astype(F32)).astype(BF16)
    return _dot(hi, exact_rhs) + _dot(lo, exact_rhs)


def _to_fp8(y):
    return jnp.clip(y, -FP8_MAX, FP8_MAX).astype(FP8)


def _rmsnorm_kernel(x_ref, g_ref, o_ref, o8_ref):
    x = x_ref[...]
    y = x * lax.rsqrt(jnp.mean(x * x, axis=-1, keepdims=True) + EPS) * g_ref[...]
    o_ref[...] = y.astype(o_ref.dtype)
    o8_ref[...] = _to_fp8(y)


def _rmsnorm(x, g, tm=256):
    t, d = x.shape
    tile = pl.BlockSpec((tm, d), lambda i: (i, 0))
    return pl.pallas_call(
        _rmsnorm_kernel,
        grid=(t // tm,),
        in_specs=[tile, pl.BlockSpec((1, d), lambda i: (0, 0))],
        out_specs=[tile, tile],
        out_shape=[jax.ShapeDtypeStruct((t, d), BF16), jax.ShapeDtypeStruct((t, d), FP8)],
        compiler_params=_cparams("parallel"),
        name="rmsnorm",
    )(x, g.reshape(1, d))


def _mm_kernel(x_ref, w_ref, *rest, has_residual, cast_weight):
    r_ref = rest[0] if has_residual else None
    o_ref = rest[1] if has_residual else rest[0]
    if cast_weight:
        wb_ref = rest[-1]

        @pl.when(pl.program_id(1) == 0)
        def _():
            wb_ref[...] = w_ref[0].astype(BF16)

        acc = _dot(x_ref[...], wb_ref[...])
    else:
        acc = _dot(x_ref[...], w_ref[0])
    if has_residual:
        acc = acc + r_ref[...]
    o_ref[...] = acc.astype(o_ref.dtype)


def _pick_tile(n, candidates):
    for c in candidates:
        if n % c == 0:
            return c
    return n


def _matmul(x, w, layer, out_dtype, col0=0, ncols=None, residual=None, name="matmul"):
    m, k = x.shape
    n = w.shape[2] - col0 if ncols is None else ncols
    cast_weight = w.dtype != BF16
    tm = _pick_tile(m, (1024, 512, 256, 128))
    tn = _pick_tile(math.gcd(n, col0) if col0 else n, (512,) if cast_weight else (1024, 768, 512, 256, 128))
    cb = col0 // tn
    in_specs = [pl.BlockSpec((tm, k), lambda j, i: (i, 0)),
                pl.BlockSpec((1, k, tn), lambda j, i: (layer, 0, cb + j))]
    args = [x, w]
    if residual is not None:
        in_specs.append(pl.BlockSpec((tm, tn), lambda j, i: (i, j)))
        args.append(residual)
    return pl.pallas_call(
        functools.partial(_mm_kernel, has_residual=residual is not None, cast_weight=cast_weight),
        grid=(n // tn, m // tm),
        in_specs=in_specs,
        out_specs=pl.BlockSpec((tm, tn), lambda j, i: (i, j)),
        out_shape=jax.ShapeDtypeStruct((m, n), out_dtype),
        scratch_shapes=[pltpu.VMEM((k, tn), BF16)] if cast_weight else [],
        compiler_params=_cparams("parallel", "arbitrary"),
        name=name,
    )(*args)


WT_TILE = 512
WT_EDGE = 32


def _mm_wt_kernel(x_ref, a_ref, *rest, shift, quantize):
    b_ref = rest[0] if shift else None
    rest = rest[1:] if shift else rest
    o_ref, wb_ref = rest[0], rest[1]
    inv_ref = rest[2] if quantize else None

    @pl.when(pl.program_id(1) == 0)
    def _():
        if shift:
            tn = a_ref.shape[1]
            w = jnp.concatenate([a_ref[0, shift:tn, :], b_ref[0, 0:shift, :]], axis=0)
        else:
            w = a_ref[0]
        if quantize:
            top = jnp.max(jnp.max(jnp.abs(w), axis=1, keepdims=True), axis=0, keepdims=True)
            top = jnp.maximum(top, 1e-30)
            wb_ref[...] = (w * (FP8_TARGET / top)).astype(FP8)
            inv_ref[...] = top * (1.0 / FP8_TARGET)
        else:
            wb_ref[...] = w.astype(BF16)

    acc = _dot_nt(x_ref[...], wb_ref[...])
    if quantize:
        acc = acc * inv_ref[...]
    if len(o_ref.shape) == 3:
        for hh in range(o_ref.shape[0]):
            o_ref[hh] = acc[:, hh * LANES:(hh + 1) * LANES].astype(o_ref.dtype)
    else:
        o_ref[...] = acc.astype(o_ref.dtype)


def _matmul_wt(x, wt, layer, out_dtype, row0, nrows, name, quantize=False, head_major=False):
    m, k = x.shape
    tn = _pick_tile(nrows, (WT_TILE, LANES))
    shift = row0 % tn
    base = row0 - shift
    assert shift % 8 == 0 and shift <= WT_EDGE and tn % WT_EDGE == 0 and base % tn == 0
    tm = _pick_tile(m, (2048, 1024, 512, 256, 128) if quantize else (1024, 512, 256, 128))
    cb = base // tn
    in_specs = [pl.BlockSpec((tm, k), lambda j, i: (i, 0)),
                pl.BlockSpec((1, tn, k), lambda j, i: (layer, cb + j, 0))]
    args = [x, wt]
    if shift:
        per = tn // WT_EDGE
        in_specs.append(pl.BlockSpec((1, WT_EDGE, k), lambda j, i: (layer, (cb + j + 1) * per, 0)))
        args.append(wt)
    return pl.pallas_call(
        functools.partial(_mm_wt_kernel, shift=shift, quantize=quantize),
        grid=(nrows // tn, m // tm),
        in_specs=in_specs,
        out_specs=(pl.BlockSpec((tn // LANES, tm, LANES), lambda j, i: (j, i, 0)) if head_major
                   else pl.BlockSpec((tm, tn), lambda j, i: (i, j))),
        out_shape=jax.ShapeDtypeStruct((nrows // LANES, m, LANES) if head_major else (m, nrows), out_dtype),
        scratch_shapes=([pltpu.VMEM((tn, k), FP8), pltpu.VMEM((1, 1), F32)] if quantize
                        else [pltpu.VMEM((tn, k), BF16)]),
        compiler_params=_cparams("parallel", "arbitrary"),
        name=name,
    )(*args)


def _hgrn2_head(qp, z, v, gp, lb, ng, st):
    ch, sb = HGRN_CHUNK, HGRN_SUB
    nsb = ch // sb
    sig = _sigmoid(z)
    logf = jnp.log(lb + (1.0 - lb) * sig)
    k = (1.0 - lb) * (1.0 - sig)
    q = qp * _sigmoid(qp)

    row = lax.broadcasted_iota(jnp.int32, (ch, ch), 0)
    col = lax.broadcasted_iota(jnp.int32, (ch, ch), 1)
    causal = row >= col
    tri = jnp.where(causal, 1.0, 0.0).astype(BF16)
    b = _split_dot(tri, logf)
    b_last = b[ch - 1:ch, :]

    o = _dot_nt((q * jnp.exp(b)).astype(BF16), st.astype(BF16))

    refs = [b[i * sb:i * sb + 1, :] for i in range(nsb)]
    rblk = jnp.concatenate([jnp.broadcast_to(r, (sb, HEAD_DIM)) for r in refs], axis=0)
    qt = q * jnp.exp(b - rblk)
    rowblk = lax.broadcasted_iota(jnp.int32, (ch, HEAD_DIM), 0) // sb
    qcat = jnp.concatenate([jnp.where(rowblk == i, qt, 0.0).astype(BF16) for i in range(nsb)], axis=1)
    kcat = jnp.concatenate(
        [(k * jnp.exp(jnp.minimum(r - b, HGRN_EXP_CLAMP))).astype(BF16) for r in refs], axis=1)
    s = jnp.where(causal, _dot_nt(qcat, kcat), 0.0)
    o = o + _dot(s.astype(BF16), v)

    kd = (k * jnp.exp(b_last - b)).astype(BF16)
    st_new = st * jnp.exp(b_last) + _dot_tn(v, kd)

    y = o * lax.rsqrt(jnp.mean(o * o, axis=-1, keepdims=True) + EPS) * ng
    return y * (gp * _sigmoid(gp)), st_new


def _hgrn2_kernel(q_ref, f_ref, i_ref, g_ref, lb_ref, ng_ref, o_ref, st_ref):
    @pl.when(pl.program_id(1) == 0)
    def _():
        st_ref[...] = jnp.zeros_like(st_ref)

    heads = range(A_HEADS)
    state = [st_ref[h] for h in heads]
    new_state = []
    for h in heads:
        cols = slice(h * HEAD_DIM, (h + 1) * HEAD_DIM)
        y, st = _hgrn2_head(q_ref[0, :, cols].astype(F32), f_ref[0, :, cols].astype(F32), i_ref[0, :, cols],
                            g_ref[0, :, cols].astype(F32), lb_ref[:, cols], ng_ref[...], state[h])
        new_state.append(st)
        o_ref[0, :, cols] = y.astype(o_ref.dtype)
    for h in heads:
        st_ref[h] = new_state[h]


def _hgrn2(h_a, lb, norm_g):
    bsz, s_len, _ = h_a.shape
    ch = HGRN_CHUNK
    a_w = A_HEADS * HEAD_DIM

    def col(j):
        return pl.BlockSpec((1, ch, a_w), lambda b, c, j=j: (b, c, j))

    return pl.pallas_call(
        _hgrn2_kernel,
        grid=(bsz, s_len // ch),
        in_specs=[col(0), col(1), col(2), col(3),
                  pl.BlockSpec((1, a_w), lambda b, c: (0, 0)),
                  pl.BlockSpec((1, HEAD_DIM), lambda b, c: (0, 0))],
        out_specs=pl.BlockSpec((1, ch, a_w), lambda b, c: (b, c, 0)),
        out_shape=jax.ShapeDtypeStruct((bsz, s_len, a_w), BF16),
        scratch_shapes=[pltpu.VMEM((A_HEADS, HEAD_DIM, HEAD_DIM), F32)],
        compiler_params=_cparams("parallel", "arbitrary"),
        name="hgrn2",
    )(h_a, h_a, h_a, h_a, lb.reshape(1, -1), norm_g.reshape(1, HEAD_DIM))


def _log_sigmoid(x):
    return jnp.minimum(x, 0.0) - jnp.log(1.0 + jnp.exp(-jnp.abs(x)))


def _causal_conv_silu(x, prev, w):
    ch = x.shape[0]
    tail = prev[ch - 8:ch, :]
    rowi = lax.broadcasted_iota(jnp.int32, tail.shape, 0)
    acc = x * w[CONV_K - 1:CONV_K, :]
    for sft in range(1, CONV_K):
        xs = pltpu.roll(x, sft, 0)
        top = jnp.where(rowi >= sft, xs[0:8, :], pltpu.roll(tail, sft, 0))
        shifted = jnp.concatenate([top, xs[8:ch, :]], axis=0)
        acc = acc + shifted * w[CONV_K - 1 - sft:CONV_K - sft, :]
    return acc * _sigmoid(acc)


def _mlstm_kernel(q_ref, k_ref, v_ref, og_ref, w_ref, gi_ref, gf_ref, ng_ref,
                  o_ref, c_ref, n_ref, m_ref, qprev_ref, kprev_ref):
    @pl.when(pl.program_id(1) == 0)
    def _():
        c_ref[...] = jnp.zeros_like(c_ref)
        n_ref[...] = jnp.zeros_like(n_ref)
        m_ref[...] = jnp.zeros_like(m_ref)
        qprev_ref[...] = jnp.zeros_like(qprev_ref)
        kprev_ref[...] = jnp.zeros_like(kprev_ref)

    ch = MLSTM_CHUNK
    b_w = B_HEADS * HEAD_DIM
    heads = range(B_HEADS)
    row = lax.broadcasted_iota(jnp.int32, (ch, ch), 0)
    col = lax.broadcasted_iota(jnp.int32, (ch, ch), 1)
    tril = row >= col
    tril_bf = jnp.where(tril, 1.0, 0.0).astype(BF16)
    triu_bf = jnp.where(row <= col, 1.0, 0.0).astype(BF16)
    eye_bf = jnp.where(row == col, 1.0, 0.0).astype(BF16)
    ones_bf = jnp.ones((ch, HEAD_DIM), BF16)

    def cols(h):
        return slice(h * HEAD_DIM, (h + 1) * HEAD_DIM)

    def nt_split(exact_lhs, x):
        hi = x.astype(BF16)
        lo = (x - hi.astype(F32)).astype(BF16)
        return _dot_nt(exact_lhs, hi) + _dot_nt(exact_lhs, lo)

    qprev = qprev_ref[...]
    kprev = kprev_ref[...]
    qc, kc, g, inter, logw_c, logw_r, m_prev, f_last = [], [], [], [], [], [], [], []
    for h in heads:
        qc.append(_causal_conv_silu(q_ref[0, :, cols(h)].astype(F32), qprev[:, cols(h)], w_ref[:, cols(h)])
                  * (HEAD_DIM ** -0.5))
        kc.append(_causal_conv_silu(k_ref[0, :, cols(h)].astype(F32), kprev[:, cols(h)],
                                    w_ref[:, b_w + h * HEAD_DIM:b_w + (h + 1) * HEAD_DIM]))
        li_r = gi_ref[0, 0, h:h + 1, :]
        lf_r = _log_sigmoid(gf_ref[0, 0, h:h + 1, :])
        fc_c = nt_split(tril_bf, jnp.broadcast_to(lf_r, (ch, ch)))
        li_c = nt_split(eye_bf, jnp.broadcast_to(li_r, (ch, ch)))
        fc_r = _split_dot_rhs(jnp.broadcast_to(lf_r, (8, ch)), triu_bf)[0:1]
        g.append(jnp.where(tril, fc_c - fc_r + li_r, NEG_BIG))
        m_prev.append(m_ref[h])
        inter.append(fc_c + m_prev[h])
        f_last.append(fc_r[:, ch - 1:ch])
        logw_c.append(f_last[h] - fc_c + li_c)
        logw_r.append(f_last[h] - fc_r + li_r)
    qprev_ref[...] = q_ref[0].astype(F32)
    kprev_ref[...] = k_ref[0].astype(F32)

    m, num, den, m_new = [], [], [], []
    for h in heads:
        m.append(jnp.maximum(jnp.max(g[h], axis=1, keepdims=True), inter[h]))
        m_new.append(jnp.maximum(f_last[h] + m_prev[h], jnp.max(logw_r[h], axis=1, keepdims=True)))
    for h in heads:
        dmat = jnp.exp(g[h] - m[h])
        a = jnp.exp(inter[h] - m[h])
        qb = qc[h].astype(BF16)
        sc = (_dot_nt(qb, kc[h].astype(BF16)) * dmat).astype(BF16)
        n_b = jnp.broadcast_to(n_ref[h], (ch, HEAD_DIM)).astype(BF16)
        num.append(_dot(sc, v_ref[0, :, cols(h)]) + a * _dot(qb, c_ref[h].astype(BF16)))
        den.append(_dot(sc, ones_bf) + a * _dot_nt(qb, n_b))

    for h in heads:
        dec = jnp.exp(f_last[h] + m_prev[h] - m_new[h])
        kw = kc[h] * jnp.exp(logw_c[h] - m_new[h])
        c_ref[h] = dec * c_ref[h] + _dot_tn(kw.astype(BF16), v_ref[0, :, cols(h)])
        n_ref[h] = dec * n_ref[h] + jnp.sum(kw, axis=0, keepdims=True)
        m_ref[h] = m_new[h]

    hs = [num[h] / jnp.maximum(jnp.abs(den[h]), jnp.exp(-m[h])) for h in heads]
    ms = [_dot((hs[h] * hs[h]).astype(BF16), ones_bf) * (1.0 / HEAD_DIM) for h in heads]
    for h in heads:
        y = hs[h] * lax.rsqrt(ms[h] + EPS) * ng_ref[...]
        o_ref[0, :, cols(h)] = (y * _sigmoid(og_ref[0, :, cols(h)].astype(F32))).astype(o_ref.dtype)


def _mlstm(h_b, conv_w, gate_i, gate_f, norm_g):
    bsz, s_len, _ = h_b.shape
    ch = MLSTM_CHUNK
    assert ch == LANES == HEAD_DIM
    nc = s_len // ch
    b_w = B_HEADS * HEAD_DIM

    def col(j):
        return pl.BlockSpec((1, ch, b_w), lambda b, c, j=j: (b, c, j))

    def rows(gt):
        return gt.reshape(bsz, nc, ch, B_HEADS).transpose(0, 1, 3, 2)

    row_spec = pl.BlockSpec((1, 1, B_HEADS, ch), lambda b, c: (b, c, 0, 0))
    return pl.pallas_call(
        _mlstm_kernel,
        grid=(bsz, nc),
        in_specs=[col(0), col(1), col(2), col(3),
                  pl.BlockSpec((CONV_K, 2 * b_w), lambda b, c: (0, 0)),
                  row_spec, row_spec,
                  pl.BlockSpec((1, HEAD_DIM), lambda b, c: (0, 0))],
        out_specs=pl.BlockSpec((1, ch, b_w), lambda b, c: (b, c, 0)),
        out_shape=jax.ShapeDtypeStruct((bsz, s_len, b_w), BF16),
        scratch_shapes=[pltpu.VMEM((B_HEADS, HEAD_DIM, HEAD_DIM), F32), pltpu.VMEM((B_HEADS, 1, HEAD_DIM), F32),
                        pltpu.VMEM((B_HEADS, 1, 1), F32), pltpu.VMEM((ch, b_w), F32),
                        pltpu.VMEM((ch, b_w), F32)],
        compiler_params=_cparams("parallel", "arbitrary"),
        name="mlstm",
    )(h_b, h_b, h_b, h_b, conv_w, rows(gate_i), rows(gate_f), norm_g.reshape(1, HEAD_DIM))


def _diff_attn_kernel(q_ref, k_ref, v_ref, cos_ref, sa_ref, sb_ref, qg_ref, kg_ref, lam_ref, sg_ref,
                      o_ref, qs_ref, kp_ref, vx_ref, *, lambda_init):
    s_len = q_ref.shape[1]
    tq = ATT_BLOCK
    nq = s_len // tq
    lane = lax.broadcasted_iota(jnp.int32, (tq, LANES), 1)
    first_map = lane < C_DQK
    li = lax.broadcasted_iota(jnp.int32, (LANES, LANES), 0) // C_DQK
    lj = lax.broadcasted_iota(jnp.int32, (LANES, LANES), 1) // C_DQK
    same_map = jnp.where(li == lj, 1.0, 0.0).astype(BF16)

    def qk_norm_rope(x, g, rows):
        ss = _split_dot_rhs(x * x, same_map)
        y = x * lax.rsqrt(ss * (1.0 / C_DQK) + EPS) * g
        return (y * cos_ref[rows, :] + pltpu.roll(y, LANES - ROPE_HALF, 1) * sa_ref[rows, :]
                + pltpu.roll(y, ROPE_HALF, 1) * sb_ref[rows, :])

    def prep(i, carry):
        rows = pl.ds(pl.multiple_of(i * tq, tq), tq)
        qn = qk_norm_rope(q_ref[0, rows, :].astype(F32), qg_ref[...], rows) * (C_DQK ** -0.5)
        qs_ref[i, 0:tq, :] = jnp.where(first_map, qn, 0.0).astype(BF16)
        qs_ref[i, tq:2 * tq, :] = jnp.where(first_map, 0.0, qn).astype(BF16)
        kp_ref[rows, :] = qk_norm_rope(k_ref[0, rows, :].astype(F32), kg_ref[...], rows).astype(BF16)
        vx_ref[rows, 0:HEAD_DIM] = v_ref[0, rows, :]
        vx_ref[rows, HEAD_DIM:2 * HEAD_DIM] = jnp.ones((tq, HEAD_DIM), BF16)
        return carry

    lax.fori_loop(0, nq, prep, 0)

    lamv = lam_ref[...]
    lam = (jnp.exp(jnp.sum(lamv[0:1] * lamv[1:2], axis=1, keepdims=True))
           - jnp.exp(jnp.sum(lamv[2:3] * lamv[3:4], axis=1, keepdims=True)) + lambda_init)
    qchunk = (lax.broadcasted_iota(jnp.int32, (2 * tq, tq), 0) % tq) // MASK_CHUNK
    kchunk = lax.broadcasted_iota(jnp.int32, (2 * tq, tq), 1) // MASK_CHUNK
    diag_ok = qchunk >= kchunk

    for i in range(nq):
        k0 = i * tq
        qs = qs_ref[i]
        s_d = jnp.where(diag_ok, _dot_nt(qs, kp_ref[k0:k0 + tq, :]), NEG_BIG)
        m = jnp.max(s_d, axis=1, keepdims=True)
        if i > 0:
            s_o = _dot_nt(qs, kp_ref[0:k0, :])
            m = jnp.maximum(m, jnp.max(s_o, axis=1, keepdims=True))
            acc = _dot(jnp.exp(s_o - m).astype(BF16), vx_ref[0:k0, :])
            acc = acc + _dot(jnp.exp(s_d - m).astype(BF16), vx_ref[k0:k0 + tq, :])
        else:
            acc = _dot(jnp.exp(s_d - m).astype(BF16), vx_ref[k0:k0 + tq, :])
        o = acc[:, 0:HEAD_DIM] / acc[:, HEAD_DIM:2 * HEAD_DIM]
        out = o[0:tq] - lam * o[tq:2 * tq]
        y = out * lax.rsqrt(jnp.mean(out * out, axis=-1, keepdims=True) + EPS) * sg_ref[...]
        o_ref[0, k0:k0 + tq, :] = (y * (1.0 - lambda_init)).astype(o_ref.dtype)


def _rope_tables(s_len):
    pos = jnp.arange(s_len, dtype=F32)
    inv_freq = ROPE_THETA ** (-jnp.arange(0, ROPE_DIM, 2, dtype=F32) / ROPE_DIM)
    ang = pos[:, None] * inv_freq[None, :]
    cos, sin = jnp.cos(ang), jnp.sin(ang)
    ones = jnp.ones((s_len, C_DQK - ROPE_DIM), F32)
    zeros = jnp.zeros((s_len, C_DQK - ROPE_DIM), F32)
    zh = jnp.zeros((s_len, ROPE_HALF), F32)
    cos_map = jnp.concatenate([cos, cos, ones], axis=1)
    sa_map = jnp.concatenate([-sin, zh, zeros], axis=1)
    sb_map = jnp.concatenate([zh, sin, zeros], axis=1)
    return tuple(jnp.concatenate([t, t], axis=1) for t in (cos_map, sa_map, sb_map))


def _diff_attn(h_c, bsz, tables, qn_g, kn_g, lam_vecs, subln_g, lambda_init):
    s_len = h_c.shape[1]

    def col(j):
        return pl.BlockSpec((1, s_len, HEAD_DIM), lambda b, h, j=j: ((j * C_HEADS + h) * bsz + b, 0, 0))

    def full(shape):
        return pl.BlockSpec(shape, lambda b, h: (0,) * len(shape))

    return pl.pallas_call(
        functools.partial(_diff_attn_kernel, lambda_init=lambda_init),
        grid=(bsz, C_HEADS),
        in_specs=[col(0), col(1), col(2), full((s_len, LANES)), full((s_len, LANES)), full((s_len, LANES)),
                  full((1, LANES)), full((1, LANES)), full((4, C_DQK)), full((1, HEAD_DIM))],
        out_specs=pl.BlockSpec((1, s_len, HEAD_DIM), lambda b, h: (b, 0, h)),
        out_shape=jax.ShapeDtypeStruct((bsz, s_len, C_HEADS * HEAD_DIM), BF16),
        scratch_shapes=[pltpu.VMEM((s_len // ATT_BLOCK, 2 * ATT_BLOCK, LANES), BF16),
                        pltpu.VMEM((s_len, LANES), BF16), pltpu.VMEM((s_len, 2 * HEAD_DIM), BF16)],
        compiler_params=_cparams("parallel", "parallel"),
        name="diff_attn",
    )(h_c, h_c, h_c, *tables, jnp.tile(qn_g, 2).reshape(1, LANES), jnp.tile(kn_g, 2).reshape(1, LANES),
      lam_vecs, subln_g.reshape(1, HEAD_DIM))


def _merge_kernel(oa_ref, ob_ref, oc_ref, wa_ref, wb_ref, wc_ref, ga_ref, gb_ref, gc_ref, y_ref,
                  wab_ref, wbb_ref, wcb_ref):
    @pl.when(pl.program_id(1) == 0)
    def _():
        wab_ref[...] = wa_ref[0].astype(BF16)
        wbb_ref[...] = wb_ref[0].astype(BF16)
        wcb_ref[...] = wc_ref[0].astype(BF16)

    y = _sigmoid(ga_ref[...].astype(F32)) * _dot(oa_ref[...], wab_ref[...])
    y = y + _sigmoid(gb_ref[...].astype(F32)) * _dot(ob_ref[...], wbb_ref[...])
    y = y + _sigmoid(gc_ref[...].astype(F32)) * _dot(oc_ref[...], wcb_ref[...])
    y_ref[...] = y.astype(y_ref.dtype)


def _merge(oa, ob, oc, wa, wb, wc, layer, gates):
    m = oa.shape[0]
    n = wa.shape[2]
    tm = _pick_tile(m, (1024, 512, 256, 128))
    tn = _pick_tile(n, (512, 256, 128))
    nb = n // tn

    def lhs(a):
        return pl.BlockSpec((tm, a.shape[1]), lambda j, i: (i, 0))

    def rhs(w):
        return pl.BlockSpec((1, w.shape[1], tn), lambda j, i: (layer, 0, j))

    def gate(k):
        return pl.BlockSpec((tm, tn), lambda j, i, k=k: (i, k * nb + j))

    return pl.pallas_call(
        _merge_kernel,
        grid=(nb, m // tm),
        in_specs=[lhs(oa), lhs(ob), lhs(oc), rhs(wa), rhs(wb), rhs(wc), gate(0), gate(1), gate(2)],
        out_specs=pl.BlockSpec((tm, tn), lambda j, i: (i, j)),
        out_shape=jax.ShapeDtypeStruct((m, n), BF16),
        scratch_shapes=[pltpu.VMEM((w.shape[1], tn), BF16) for w in (wa, wb, wc)],
        compiler_params=_cparams("parallel", "arbitrary"),
        name="merge",
    )(oa, ob, oc, wa, wb, wc, gates, gates, gates)


def _pack_bf16_pairs(x):
    n = x.shape[1] // 2
    lo = lax.bitcast_convert_type(x[:, :n].astype(BF16).astype(F32), jnp.uint32)
    hi = lax.bitcast_convert_type(x[:, n:].astype(BF16).astype(F32), jnp.uint32)
    return (hi & jnp.uint32(0xFFFF0000)) | (lo >> 16)


def _unpack_bf16_pairs(u):
    lo = lax.bitcast_convert_type(u << 16, F32)
    hi = lax.bitcast_convert_type(u & jnp.uint32(0xFFFF0000), F32)
    return lo, hi


def _router_kernel(x_ref, g_ref, w_ref, b_ref, xq_ref, route_ref):
    x = x_ref[...]
    xn = x * lax.rsqrt(jnp.mean(x * x, axis=-1, keepdims=True) + EPS) * g_ref[...]
    xq_ref[...] = _pack_bf16_pairs(xn)
    logits = jnp.dot(xn, w_ref[...], preferred_element_type=F32, precision=lax.Precision.HIGHEST) + b_ref[...]
    lane = lax.broadcasted_iota(jnp.int32, logits.shape, 1).astype(F32)
    none = float(LANES)

    def first_argmax(vals):
        top = jnp.max(vals, axis=1, keepdims=True)
        return top, jnp.min(jnp.where(vals == top, lane, none), axis=1, keepdims=True)

    is_group = lane < N_GROUPS
    g_top, g_idx = first_argmax(jnp.where(is_group, logits, NEG_BIG))
    g_w = 1.0 / jnp.sum(jnp.where(is_group, jnp.exp(logits - g_top), 0.0), axis=1, keepdims=True)
    lo = N_GROUPS + EXPERTS_PER_GROUP * g_idx
    e_logits = jnp.where((lane >= lo) & (lane < lo + EXPERTS_PER_GROUP), logits, NEG_BIG)
    v1, i1 = first_argmax(e_logits)
    v2, i2 = first_argmax(jnp.where(lane == i1, NEG_BIG, e_logits))
    e2 = jnp.exp(v2 - v1)
    w1 = g_w / (1.0 + e2)
    w2 = w1 * e2
    route = jnp.where(lane == 0, i1 - N_GROUPS, jnp.where(lane == 1, i2 - N_GROUPS,
                      jnp.where(lane == 2, w1, jnp.where(lane == 3, w2, 0.0))))
    route_ref[...] = route


def _router(x, g, w_router, b_router, tm=256):
    t, d = x.shape
    return pl.pallas_call(
        _router_kernel,
        grid=(t // tm,),
        in_specs=[pl.BlockSpec((tm, d), lambda i: (i, 0)), pl.BlockSpec((1, d), lambda i: (0, 0)),
                  pl.BlockSpec((d, LANES), lambda i: (0, 0)), pl.BlockSpec((1, LANES), lambda i: (0, 0))],
        out_specs=[pl.BlockSpec((tm, d // 2), lambda i: (i, 0)), pl.BlockSpec((tm, LANES), lambda i: (i, 0))],
        out_shape=[jax.ShapeDtypeStruct((t, d // 2), jnp.uint32), jax.ShapeDtypeStruct((t, LANES), F32)],
        compiler_params=_cparams("parallel"),
        name="router",
    )(x, g.reshape(1, d), w_router, b_router)


def _gather_rows(src_hbm, dst_ref, idx_ref, base, n_rows, sem):
    def body(j, carry):
        for u in range(GATHER_UNROLL):
            r = j * GATHER_UNROLL + u
            pltpu.make_async_copy(src_hbm.at[pl.ds(idx_ref[base + r], 1)], dst_ref.at[pl.ds(r, 1)],
                                  sem).start(priority=u % 2)
        return carry
    lax.fori_loop(0, n_rows // GATHER_UNROLL, body, 0)


def _wait_rows(src_hbm, dst_ref, n_rows, sem):
    pltpu.make_async_copy(src_hbm.at[pl.ds(0, n_rows)], dst_ref, sem).wait()


def _expert_kernel(tile_expert_ref, row_token_ref, n_used_ref, x_hbm, w1_ref, w3_ref, w2_ref, y_ref, xbuf, sem):
    del tile_expert_ref
    i = pl.program_id(0)
    n_used = n_used_ref[0]
    tm = MOE_TILE
    half = w1_ref.shape[1] // 2
    slot = i % 2

    @pl.when((i == 0) & (n_used > 0))
    def _():
        _gather_rows(x_hbm, xbuf.at[0], row_token_ref, 0, tm, sem.at[0])

    @pl.when(i + 1 < n_used)
    def _():
        _gather_rows(x_hbm, xbuf.at[1 - slot], row_token_ref, (i + 1) * tm, tm, sem.at[1 - slot])

    @pl.when(i < n_used)
    def _():
        _wait_rows(x_hbm, xbuf.at[slot], tm, sem.at[slot])
        lo, hi = _unpack_bf16_pairs(xbuf[slot])
        lo = lo.astype(BF16)
        hi = hi.astype(BF16)
        h1 = _dot(lo, w1_ref[0, 0:half, :]) + _dot(hi, w1_ref[0, half:2 * half, :])
        h3 = _dot(lo, w3_ref[0, 0:half, :]) + _dot(hi, w3_ref[0, half:2 * half, :])
        h = (h1 * _sigmoid(h1) * h3).astype(BF16)
        y_ref[...] = _pack_bf16_pairs(_dot(h, w2_ref[0]))

    @pl.when(i >= n_used)
    def _():
        y_ref[...] = jnp.zeros_like(y_ref)


def _experts(xq, w1, w3, w2, layer, tile_expert, row_token, n_used):
    t, dh = xq.shape
    d = 2 * dh
    ff = w1.shape[2]
    tm = MOE_TILE
    n_tiles = tile_expert.shape[0]
    e0 = layer * N_EXPERTS
    grid_spec = pltpu.PrefetchScalarGridSpec(
        num_scalar_prefetch=3,
        grid=(n_tiles,),
        in_specs=[pl.BlockSpec(memory_space=pl.ANY),
                  pl.BlockSpec((1, d, ff), lambda i, te, rt, nu: (e0 + te[i], 0, 0)),
                  pl.BlockSpec((1, d, ff), lambda i, te, rt, nu: (e0 + te[i], 0, 0)),
                  pl.BlockSpec((1, ff, d), lambda i, te, rt, nu: (e0 + te[i], 0, 0))],
        out_specs=pl.BlockSpec((tm, dh), lambda i, te, rt, nu: (i, 0)),
        scratch_shapes=[pltpu.VMEM((2, tm, dh), jnp.uint32), pltpu.SemaphoreType.DMA((2,))],
    )
    return pl.pallas_call(
        _expert_kernel,
        grid_spec=grid_spec,
        out_shape=jax.ShapeDtypeStruct((n_tiles * tm, dh), jnp.uint32),
        compiler_params=_cparams("arbitrary"),
        name="experts",
    )(tile_expert, row_token, n_used, xq, w1, w3, w2)


def _combine_kernel(pos_ref, y_hbm, x_ref, route_ref, *rest):
    if len(rest) == 6:
        g_ref, xo_ref, xn_ref, x8_ref, ybuf, sem = rest
    else:
        g_ref, xn_ref, x8_ref = None, None, None
        xo_ref, ybuf, sem = rest
    i = pl.program_id(0)
    n = pl.num_programs(0)
    tc = COMBINE_TILE
    slot = i % 2

    @pl.when(i == 0)
    def _():
        _gather_rows(y_hbm, ybuf.at[0], pos_ref, 0, 2 * tc, sem.at[0])

    @pl.when(i + 1 < n)
    def _():
        _gather_rows(y_hbm, ybuf.at[1 - slot], pos_ref, (i + 1) * 2 * tc, 2 * tc, sem.at[1 - slot])

    _wait_rows(y_hbm, ybuf.at[slot], 2 * tc, sem.at[slot])
    lo0, hi0 = _unpack_bf16_pairs(ybuf[slot, 0:tc, :])
    lo1, hi1 = _unpack_bf16_pairs(ybuf[slot, tc:2 * tc, :])
    w0 = route_ref[:, 2:3]
    w1 = route_ref[:, 3:4]
    x = x_ref[...] + jnp.concatenate([w0 * lo0 + w1 * lo1, w0 * hi0 + w1 * hi1], axis=1)
    xo_ref[...] = x
    if xn_ref is not None:
        y = x * lax.rsqrt(jnp.mean(x * x, axis=-1, keepdims=True) + EPS) * g_ref[...]
        xn_ref[...] = y.astype(xn_ref.dtype)
        x8_ref[...] = _to_fp8(y)


def _combine(y_sorted, x, route, pos_tiles, next_g=None):
    t, d = x.shape
    tc = COMBINE_TILE
    tile = pl.BlockSpec((tc, d), lambda i, p: (i, 0))
    in_specs = [pl.BlockSpec(memory_space=pl.ANY), tile, pl.BlockSpec((tc, LANES), lambda i, p: (i, 0))]
    out_specs = [tile]
    out_shape = [jax.ShapeDtypeStruct((t, d), F32)]
    args = [pos_tiles, y_sorted, x, route]
    if next_g is not None:
        in_specs.append(pl.BlockSpec((1, d), lambda i, p: (0, 0)))
        out_specs += [tile, tile]
        out_shape += [jax.ShapeDtypeStruct((t, d), BF16), jax.ShapeDtypeStruct((t, d), FP8)]
        args.append(next_g.reshape(1, d))
    grid_spec = pltpu.PrefetchScalarGridSpec(
        num_scalar_prefetch=1,
        grid=(t // tc,),
        in_specs=in_specs,
        out_specs=out_specs,
        scratch_shapes=[pltpu.VMEM((2, 2 * tc, d // 2), jnp.uint32), pltpu.SemaphoreType.DMA((2,))],
    )
    out = pl.pallas_call(
        _combine_kernel,
        grid_spec=grid_spec,
        out_shape=out_shape,
        compiler_params=_cparams("arbitrary"),
        name="combine",
    )(*args)
    return tuple(out) if next_g is not None else (out[0], None, None)


def _route_plan(expert_ids, n_tokens):
    tm, tc = MOE_TILE, COMBINE_TILE
    n_assign = 2 * n_tokens
    n_tiles = n_assign // tm + N_EXPERTS
    e_flat = expert_ids.reshape(-1)
    onehot = (e_flat[:, None] == jnp.arange(N_EXPERTS, dtype=jnp.int32)[None, :]).astype(jnp.int32)
    csum = jnp.cumsum(onehot, axis=0)
    rank = jnp.sum(csum * onehot, axis=1) - 1
    counts = csum[-1]
    padded = ((counts + tm - 1) // tm) * tm
    ends = jnp.cumsum(padded)
    offs = ends - padded
    pos = jnp.sum(onehot * offs[None, :], axis=1) + rank
    row_token = jnp.zeros((n_tiles * tm,), jnp.int32).at[pos].set(jnp.arange(n_assign, dtype=jnp.int32) // 2)
    tile_start = jnp.arange(n_tiles, dtype=jnp.int32) * tm
    tile_expert = jnp.minimum(jnp.sum((ends[None, :] <= tile_start[:, None]).astype(jnp.int32), axis=1),
                              N_EXPERTS - 1)
    pos_tiles = pos.reshape(n_tokens // tc, tc, 2).transpose(0, 2, 1).reshape(-1)
    n_used = (ends[-1:] // tm).astype(jnp.int32)
    return tile_expert, row_token, n_used, pos_tiles


def kernel(x, norm1_g, w_in, hgrn_lb, hgrn_norm_g, mlstm_conv_w, mlstm_i_bias, mlstm_f_bias, mlstm_norm_g,
           diff_qn_g, diff_kn_g, diff_lam_q1, diff_lam_k1, diff_lam_q2, diff_lam_k2, diff_subln_g, w_branch_a,
           w_branch_b, w_branch_c, w_out, norm2_g, router_group_w, router_group_b, router_expert_w,
           router_expert_b, expert_w1, expert_w3, expert_w2):
    bsz, s_len, d = x.shape
    depth = w_in.shape[0]
    t = bsz * s_len
    a_w, b_w, c_w = A_HEADS * HEAD_DIM, B_HEADS * HEAD_DIM, C_HEADS * HEAD_DIM
    o_b = 4 * a_w
    o_s = o_b + 4 * b_w
    o_c = o_s + 2 * B_HEADS
    o_g = o_c + 3 * c_w

    tables = _rope_tables(s_len)
    p_lb = jax.nn.softmax(hgrn_lb.astype(F32), axis=0)
    lb_all = jnp.cumsum(p_lb, axis=0) - p_lb[0:1]

    w_in_t = jnp.swapaxes(w_in, 1, 2)
    ff = expert_w1.shape[-1]
    ew1 = expert_w1.astype(BF16).reshape(depth * N_EXPERTS, d, ff)
    ew3 = expert_w3.astype(BF16).reshape(depth * N_EXPERTS, d, ff)
    ew2 = expert_w2.astype(BF16).reshape(depth * N_EXPERTS, ff, d)

    xf = x.reshape(t, d)
    xn, xn8 = _rmsnorm(xf, norm1_g[0])
    for l in range(depth):
        h_a = _matmul_wt(xn, w_in_t, l, BF16, 0, o_b, "in_proj_a").reshape(bsz, s_len, -1)
        h_b = _matmul_wt(xn, w_in_t, l, BF16, o_b, o_s - o_b, "in_proj_b").reshape(bsz, s_len, -1)
        h_c = _matmul_wt(xn, w_in_t, l, BF16, o_c, o_g - o_c, "in_proj_c", head_major=True)
        h_c = h_c.reshape(-1, s_len, HEAD_DIM)
        gates = _matmul_wt(xn8, w_in_t, l, BF16, o_g, w_in.shape[2] - o_g, "in_proj_gates", quantize=True)
        h_s = _matmul_wt(xn, w_in_t, l, F32, o_s, LANES, "in_proj_scalar_gates").reshape(bsz, s_len, LANES)
        gate_i = h_s[..., :B_HEADS] + mlstm_i_bias[l]
        gate_f = h_s[..., B_HEADS:2 * B_HEADS] + mlstm_f_bias[l]

        out_a = _hgrn2(h_a, lb_all[l], hgrn_norm_g[l])
        out_b = _mlstm(h_b, mlstm_conv_w[l], gate_i, gate_f, mlstm_norm_g[l])
        lambda_init = 0.8 - 0.6 * math.exp(-0.3 * l)
        lam_vecs = jnp.stack([diff_lam_q1[l], diff_lam_k1[l], diff_lam_q2[l], diff_lam_k2[l]]).astype(F32)
        out_c = _diff_attn(h_c, bsz, tables, diff_qn_g[l], diff_kn_g[l], lam_vecs, diff_subln_g[l], lambda_init)

        y = _merge(out_a.reshape(t, a_w), out_b.reshape(t, b_w), out_c.reshape(t, c_w),
                   w_branch_a, w_branch_b, w_branch_c, l, gates)
        xf = _matmul(y, w_out, l, F32, residual=xf, name="out_proj")

        n_router = N_GROUPS + N_EXPERTS
        w_router = jnp.pad(jnp.concatenate([router_group_w[l], router_expert_w[l]], axis=1),
                           ((0, 0), (0, LANES - n_router)))
        b_router = jnp.pad(jnp.concatenate([router_group_b[l], router_expert_b[l]]),
                           (0, LANES - n_router)).reshape(1, LANES)
        xq, route = _router(xf, norm2_g[l], w_router, b_router)
        tile_expert, row_token, n_used, pos_tiles = _route_plan(route[:, 0:2].astype(jnp.int32), t)
        y_sorted = _experts(xq, ew1, ew3, ew2, l, tile_expert, row_token, n_used)
        xf, xn, xn8 = _combine(y_sorted, xf, route, pos_tiles, norm1_g[l + 1] if l + 1 < depth else None)
    return xf.reshape(bsz, s_len, d)
```
